```python
import math
import jax, jax.numpy as jnp
from jax import lax
import numpy as np

D_MODEL = 4096
BATCH = 2
SEQ = 8192
DEPTH = 2
DEC_BATCH = 1
DEC_SEQ = 16384
PAST_LEN = 128

D_FF = D_MODEL // 2
D_INNER = D_MODEL // 2
SSM_HEAD_DIM = 64
SSM_HEADS = D_INNER // SSM_HEAD_DIM
SSM_GROUPS = 4
SSM_HEADS_PER_GROUP = SSM_HEADS // SSM_GROUPS
SSM_STATE = 128
CONV_WIDTH = 5
CONV_CH = D_INNER + 2 * SSM_GROUPS * SSM_STATE
SSD_CHUNK = 128
DIFF_HEADS = 4
DIFF_HEAD_DIM = 128
DIFF_WIDTH = DIFF_HEADS * 2 * DIFF_HEAD_DIM
NUM_BUCKETS = 32
MAX_DISTANCE = 128
Q_BLOCK = 128
GMLP_WIDTH = D_MODEL // 4
GMLP_GROUPS = 4
GMLP_GROUP_DIM = GMLP_WIDTH // GMLP_GROUPS
GMLP_CHUNK = 128
N_BRANCHES = 3
SSM_COLS = D_INNER + CONV_CH + 2 * SSM_HEADS
DIFF_COLS = 3 * DIFF_WIDTH
GMLP_COLS = 2 * GMLP_WIDTH
GATE_COLS = N_BRANCHES * D_MODEL
IN_COLS = SSM_COLS + DIFF_COLS + GMLP_COLS + GATE_COLS
NORM_EPS = 1e-6

kernel_name = 'hybrid_bidir_ssd_diffattn_gmlp_encoder'

f32 = jnp.float32


def rms_norm(x, g, eps=NORM_EPS):
    xf = x.astype(f32)
    y = xf * lax.rsqrt(jnp.mean(jnp.square(xf), axis=-1, keepdims=True) + eps)
    return (y * g.astype(f32)).astype(x.dtype)


def layer_norm(x, g, b, eps=1e-5):
    xf = x.astype(f32)
    mu = jnp.mean(xf, axis=-1, keepdims=True)
    xc = xf - mu
    y = xc * lax.rsqrt(jnp.mean(jnp.square(xc), axis=-1, keepdims=True) + eps)
    return (y * g.astype(f32) + b.astype(f32)).astype(x.dtype)


def swiglu_ffn(h, w_in, w_out):
    gate, up = jnp.split(h @ w_in, 2, axis=-1)
    return (jax.nn.silu(gate) * up) @ w_out


def centred_depthwise_conv(x, w, b):
    pad = CONV_WIDTH // 2
    y = lax.conv_general_dilated(x, w[:, None, :].astype(x.dtype), window_strides=(1,),
                                 padding=[(pad, pad)], dimension_numbers=('NWC', 'WIO', 'NWC'),
                                 feature_group_count=x.shape[-1])
    return y + b.astype(x.dtype)


def ssd_chunked(x, dt, a, bm, cm):
    bsz, seq = x.shape[:2]
    nc = seq // SSD_CHUNK

    def chunk(t):
        return t.reshape((bsz, nc, SSD_CHUNK) + t.shape[2:])

    x, dt, bm, cm = chunk(x), chunk(dt), chunk(bm), chunk(cm)
    a_cum = jnp.cumsum(dt * a, axis=2)
    lower = jnp.tril(jnp.ones((SSD_CHUNK, SSD_CHUNK), bool))[:, :, None, None]
    seg = a_cum[:, :, :, None] - a_cum[:, :, None, :]
    cb = jnp.einsum('bclgn,bcsgn->bclsg', cm, bm).astype(f32)
    m = cb[..., None] * jnp.exp(jnp.where(lower, seg, -jnp.inf)) * dt[:, :, None]
    y_diag = jnp.einsum('bclsgh,bcsghp->bclghp', m.astype(x.dtype), x)
    to_end = jnp.exp(a_cum[:, :, -1:] - a_cum) * dt
    states = jnp.einsum('bcsgn,bcsghp->bcghpn', bm,
                        (x * to_end[..., None].astype(x.dtype))).astype(f32)
    chunk_decay = jnp.exp(a_cum[:, :, -1])

    def step(h, inp):
        st, dec = inp
        return h * dec[..., None, None] + st, h

    h0 = jnp.zeros(states.shape[:1] + states.shape[2:], f32)
    _, h_in = lax.scan(step, h0, (jnp.moveaxis(states, 1, 0), jnp.moveaxis(chunk_decay, 1, 0)))
    h_in = jnp.moveaxis(h_in, 0, 1)
    y_off = (jnp.einsum('bclgn,bcghpn->bclghp', cm.astype(f32), h_in)
             * jnp.exp(a_cum)[..., None])
    return (y_diag.astype(f32) + y_off).reshape((bsz, seq) + y_diag.shape[3:])


def ssm_branch(zxbcdt, conv_w, conv_b, dt_bias, a_log, d_skip, norm_w, w_o):
    bsz, seq = zxbcdt.shape[:2]
    z, xbc, dt_raw = jnp.split(zxbcdt, [D_INNER, D_INNER + CONV_CH], axis=-1)
    xbc = jax.nn.silu(centred_depthwise_conv(xbc, conv_w, conv_b))
    xs, bm, cm = jnp.split(xbc, [D_INNER, D_INNER + SSM_GROUPS * SSM_STATE], axis=-1)
    xs = xs.reshape(bsz, seq, SSM_GROUPS, SSM_HEADS_PER_GROUP, SSM_HEAD_DIM)
    bm = bm.reshape(bsz, seq, SSM_GROUPS, SSM_STATE)
    cm = cm.reshape(bsz, seq, SSM_GROUPS, SSM_STATE)
    dt = jax.nn.softplus(dt_raw.astype(f32).reshape(bsz, seq, 2, SSM_GROUPS, SSM_HEADS_PER_GROUP)
                         + dt_bias.astype(f32).reshape(2, SSM_GROUPS, SSM_HEADS_PER_GROUP))
    a = -jnp.exp(a_log.astype(f32)).reshape(2, SSM_GROUPS, SSM_HEADS_PER_GROUP)

    def flip(t):
        return jnp.flip(t, axis=1)

    y = (ssd_chunked(xs, dt[:, :, 0], a[0], bm, cm)
         + flip(ssd_chunked(flip(xs), flip(dt[:, :, 1]), a[1], flip(bm), flip(cm)))
         + d_skip.astype(f32).reshape(SSM_GROUPS, SSM_HEADS_PER_GROUP, 1) * xs.astype(f32))
    y = y.reshape(bsz, seq, D_INNER) * jax.nn.silu(z.astype(f32))
    yg = y.reshape(bsz, seq, SSM_GROUPS, D_INNER // SSM_GROUPS)
    yg = yg * lax.rsqrt(jnp.mean(jnp.square(yg), axis=-1, keepdims=True) + NORM_EPS)
    y = (yg.reshape(bsz, seq, D_INNER) * norm_w.astype(f32)).astype(zxbcdt.dtype)
    return y @ w_o


def t5_relative_bucket(rel):
    half = NUM_BUCKETS // 2
    max_exact = half // 2
    n = jnp.abs(rel)
    log_ratio = jnp.log(jnp.maximum(n, 1).astype(f32) / max_exact) / math.log(MAX_DISTANCE / max_exact)
    large = jnp.minimum(max_exact + (log_ratio * (half - max_exact)).astype(jnp.int32), half - 1)
    return jnp.where(rel > 0, half, 0) + jnp.where(n < max_exact, n, large)


def diff_attention_branch(qkv, rel_bias, lambda_qk, subln_w, w_o, lambda_init):
    bsz, seq = qkv.shape[:2]
    q, k, v = jnp.split(qkv, 3, axis=-1)
    q = q.reshape(bsz, seq, DIFF_HEADS, 2, DIFF_HEAD_DIM) * (DIFF_HEAD_DIM ** -0.5)
    k = k.reshape(bsz, seq, DIFF_HEADS, 2, DIFF_HEAD_DIM)
    v = v.reshape(bsz, seq, DIFF_HEADS, 2 * DIFF_HEAD_DIM)
    lq = lambda_qk.astype(f32)
    lam = jnp.exp(jnp.sum(lq[0] * lq[1])) - jnp.exp(jnp.sum(lq[2] * lq[3])) + lambda_init
    n_blocks = seq // Q_BLOCK
    q_blocks = jnp.moveaxis(q.reshape(bsz, n_blocks, Q_BLOCK, DIFF_HEADS, 2, DIFF_HEAD_DIM), 1, 0)
    k_pos = jnp.arange(seq)
    table_t = jnp.transpose(rel_bias.astype(f32), (1, 0))

    def attend(args):
        q_blk, blk = args
        q_pos = blk * Q_BLOCK + jnp.arange(Q_BLOCK)
        bias = table_t[:, t5_relative_bucket(k_pos[None, :] - q_pos[:, None])]
        logits = (jnp.einsum('bqhjd,bkhjd->bhjqk', q_blk, k).astype(f32)
                  + bias[None, :, None])
        e = jnp.exp(logits - jnp.max(logits, axis=-1, keepdims=True))
        denom = jnp.transpose(jnp.sum(e, axis=-1), (0, 3, 1, 2))[..., None]
        o = jnp.einsum('bhjqk,bkhe->bqhje', e.astype(v.dtype), v).astype(f32) / denom
        return (o[:, :, :, 0] - lam * o[:, :, :, 1]).astype(v.dtype)

    o = lax.map(attend, (q_blocks, jnp.arange(n_blocks)))
    o = jnp.moveaxis(o, 0, 1).reshape(bsz, seq, DIFF_HEADS, 2 * DIFF_HEAD_DIM)
    o = rms_norm(o, subln_w, 1e-5) * (1.0 - lambda_init)
    return o.reshape(bsz, seq, DIFF_WIDTH) @ w_o


def spatial_gating_branch(uv, ln_v, w_s, b_s, w_o):
    bsz, seq = uv.shape[:2]
    u, v = jnp.split(jax.nn.gelu(uv), 2, axis=-1)
    v = layer_norm(v, ln_v[0], ln_v[1])
    nc = seq // GMLP_CHUNK
    v = v.reshape(bsz, nc, GMLP_CHUNK, GMLP_GROUPS, GMLP_GROUP_DIM)
    mixed = (jnp.einsum('gts,bcsgd->bctgd', w_s.astype(v.dtype), v)
             + jnp.transpose(b_s, (1, 0))[:, :, None].astype(v.dtype))
    return (u * mixed.reshape(bsz, seq, GMLP_WIDTH)) @ w_o


def trunk(x, rel_bias, norms, w_ffn1_in, w_ffn1_out, w_in, conv_w, conv_b, dt_bias, a_log, d_skip,
          ssm_norm, w_o_ssm, lambda_qk, diff_subln, w_o_diff, ln_v, w_spatial, b_spatial, w_o_gmlp,
          w_out, w_ffn2_in, w_ffn2_out):
    splits = [SSM_COLS, SSM_COLS + DIFF_COLS, SSM_COLS + DIFF_COLS + GMLP_COLS]
    for l in range(DEPTH):
        n = norms[l]
        lambda_init = 0.8 - 0.6 * math.exp(-0.3 * l)
        x = x + 0.5 * rms_norm(swiglu_ffn(rms_norm(x, n[0]), w_ffn1_in[l], w_ffn1_out[l]), n[1])
        h = rms_norm(x, n[2])
        ssm_in, diff_in, gmlp_in, gate_in = jnp.split(h @ w_in[l], splits, axis=-1)
        gates = jax.nn.sigmoid(gate_in.astype(f32)).reshape(x.shape[:2] + (N_BRANCHES, D_MODEL))
        y_ssm = ssm_branch(ssm_in, conv_w[l], conv_b[l], dt_bias[l], a_log[l], d_skip[l],
                           ssm_norm[l], w_o_ssm[l])
        y_diff = diff_attention_branch(diff_in, rel_bias, lambda_qk[l], diff_subln[l], w_o_diff[l],
                                       lambda_init)
        y_gmlp = spatial_gating_branch(gmlp_in, ln_v[l], w_spatial[l], b_spatial[l], w_o_gmlp[l])
        merged = gates[..., 0, :] * y_ssm + gates[..., 1, :] * y_diff + gates[..., 2, :] * y_gmlp
        x = x + rms_norm(merged.astype(x.dtype) @ w_out[l], n[3])
        x = x + 0.5 * rms_norm(swiglu_ffn(rms_norm(x, n[4]), w_ffn2_in[l], w_ffn2_out[l]), n[5])
    return x


def setup_inputs(seed: int = 0) -> dict:
    key = jax.random.key(seed)
    ks = jax.random.split(key, 26)

    def nrm(k, shape, scale):
        return jax.random.normal(k, shape, f32) * scale

    dt0 = jnp.exp(jax.random.uniform(ks[8], (DEPTH, 2, SSM_HEADS), f32,
                                     minval=math.log(1e-3), maxval=math.log(1e-1)))
    return {
        'x_prompt': nrm(ks[0], (BATCH, SEQ, D_MODEL), 1.0),
        'x_sample': nrm(ks[1], (DEC_BATCH, DEC_SEQ, D_MODEL), 1.0),
        'rel_bias': nrm(ks[2], (NUM_BUCKETS, DIFF_HEADS), 0.5),
        'norms': 1.0 + nrm(ks[3], (DEPTH, 6, D_MODEL), 0.1),
        'w_ffn1_in': nrm(ks[4], (DEPTH, D_MODEL, 2 * D_FF), D_MODEL ** -0.5),
        'w_ffn1_out': nrm(ks[5], (DEPTH, D_FF, D_MODEL), D_FF ** -0.5),
        'w_in': nrm(ks[6], (DEPTH, D_MODEL, IN_COLS), D_MODEL ** -0.5),
        'conv_w': nrm(ks[7], (DEPTH, CONV_WIDTH, CONV_CH), CONV_WIDTH ** -0.5),
        'conv_b': nrm(ks[9], (DEPTH, CONV_CH), 0.02),
        'dt_bias': dt0 + jnp.log(-jnp.expm1(-dt0)),
        'a_log': jnp.log(jax.random.uniform(ks[10], (DEPTH, 2, SSM_HEADS), f32, minval=1.0, maxval=16.0)),
        'd_skip': 1.0 + nrm(ks[11], (DEPTH, SSM_HEADS), 0.1),
        'ssm_norm': 1.0 + nrm(ks[12], (DEPTH, D_INNER), 0.1),
        'w_o_ssm': nrm(ks[13], (DEPTH, D_INNER, D_MODEL), D_INNER ** -0.5),
        'lambda_qk': nrm(ks[14], (DEPTH, 4, DIFF_HEAD_DIM), 0.1),
        'diff_subln': 1.0 + nrm(ks[15], (DEPTH, 2 * DIFF_HEAD_DIM), 0.1),
        'w_o_diff': nrm(ks[16], (DEPTH, DIFF_WIDTH, D_MODEL), DIFF_WIDTH ** -0.5),
        'ln_v': jnp.stack([1.0 + nrm(ks[17], (DEPTH, GMLP_WIDTH), 0.1),
                           nrm(ks[18], (DEPTH, GMLP_WIDTH), 0.02)], axis=1),
        'w_spatial': nrm(ks[19], (DEPTH, GMLP_GROUPS, GMLP_CHUNK, GMLP_CHUNK), GMLP_CHUNK ** -0.5),
        'b_spatial': 1.0 + nrm(ks[20], (DEPTH, GMLP_GROUPS, GMLP_CHUNK), 0.02),
        'w_o_gmlp': nrm(ks[21], (DEPTH, GMLP_WIDTH, D_MODEL), GMLP_WIDTH ** -0.5),
        'w_out': nrm(ks[22], (DEPTH, D_MODEL, D_MODEL), D_MODEL ** -0.5),
        'w_ffn2_in': nrm(ks[23], (DEPTH, D_MODEL, 2 * D_FF), D_MODEL ** -0.5),
        'w_ffn2_out': nrm(ks[24], (DEPTH, D_FF, D_MODEL), D_FF ** -0.5),
    }


def reference(x_prompt, x_sample, rel_bias, norms, w_ffn1_in, w_ffn1_out, w_in, conv_w, conv_b, dt_bias,
              a_log, d_skip, ssm_norm, w_o_ssm, lambda_qk, diff_subln, w_o_diff, ln_v, w_spatial, b_spatial,
              w_o_gmlp, w_out, w_ffn2_in, w_ffn2_out):
    y_prompt = trunk(x_prompt, rel_bias, norms, w_ffn1_in, w_ffn1_out, w_in, conv_w, conv_b, dt_bias, a_log,
                     d_skip, ssm_norm, w_o_ssm, lambda_qk, diff_subln, w_o_diff, ln_v, w_spatial, b_spatial,
                     w_o_gmlp, w_out, w_ffn2_in, w_ffn2_out)
    y_sample = trunk(x_sample, rel_bias, norms, w_ffn1_in, w_ffn1_out, w_in, conv_w, conv_b, dt_bias, a_log,
                     d_skip, ssm_norm, w_o_ssm, lambda_qk, diff_subln, w_o_diff, ln_v, w_spatial, b_spatial,
                     w_o_gmlp, w_out, w_ffn2_in, w_ffn2_out)
    return (y_prompt, y_sample)
```

```python
import functools
import math

import jax
import jax.numpy as jnp
from jax import lax
from jax.experimental import pallas as pl
from jax.experimental.pallas import tpu as pltpu

F32 = jnp.float32
BF16 = jnp.bfloat16

SSM_HEAD_DIM = 64
SSM_STATE = 128
SSD_CHUNK = 128
DIFF_HEAD_DIM = 128
NUM_BUCKETS = 32
MAX_DISTANCE = 128
GMLP_CHUNK = 128
N_BRANCHES = 3
NORM_EPS = 1e-6
SUBLN_EPS = 1e-5
LN_EPS = 1e-5

LANES = 128
BF16_SUBLANES = 16
VMEM_LIMIT_BYTES = 56 * 1024 * 1024

MM_TM = 1024
MM_TN = 1024
ROW_TILE = 256
CONV_ROWS = 512
CONV_COLS = 512
ATTN_TILE = 512
MERGE_TM = 512
MERGE_TN = 256
GMLP_ROWS = 256


def _cparams(sem):
    return pltpu.CompilerParams(dimension_semantics=sem, vmem_limit_bytes=VMEM_LIMIT_BYTES)


def _dot(a, b):
    return jnp.dot(a, b, preferred_element_type=F32)


def _dot_nt(a, b):
    return lax.dot_general(a, b, (((1,), (1,)), ((), ())), preferred_element_type=F32)


def _dot_tn(a, b):
    return lax.dot_general(a, b, (((0,), (0,)), ((), ())), preferred_element_type=F32)


def _split_bf16(x, n):
    parts = []
    r = x
    for _ in range(n):
        p = r.astype(BF16)
        parts.append(p)
        r = r - p.astype(F32)
    return parts


def _rms(x, eps):
    return x * lax.rsqrt(jnp.mean(x * x, axis=-1, keepdims=True) + eps)


def _silu(x):
    return x * jax.nn.sigmoid(x)


def _softplus(x):
    return jnp.maximum(x, 0.0) + jnp.log1p(jnp.exp(-jnp.abs(x)))


def _mm_scale_kernel(a_ref, w_ref, s_ref, o_ref):
    o_ref[...] = (_dot(a_ref[...], w_ref[...]) * s_ref[...]).astype(o_ref.dtype)


def _mm_kernel(a_ref, w_ref, o_ref):
    o_ref[...] = _dot(a_ref[...], w_ref[...]).astype(o_ref.dtype)


def _mm_swiglu_kernel(a_ref, wg_ref, wu_ref, o_ref):
    a = a_ref[...]
    g = _dot(a, wg_ref[...])
    u = _dot(a, wu_ref[...])
    o_ref[...] = (_silu(g) * u).astype(o_ref.dtype)


def _mm_t_kernel(w_ref, a_ref, o_ref):
    o_ref[...] = _dot_nt(w_ref[...], a_ref[...]).astype(o_ref.dtype)


def _matmul(a, w, out_dtype, scale=None, tm=None, tn=None):
    m, k = a.shape
    n = w.shape[1]
    tm = min(tm or MM_TM, m)
    tn = min(tn or MM_TN, n)
    in_specs = [pl.BlockSpec((tm, k), lambda i, j: (i, 0)), pl.BlockSpec((k, tn), lambda i, j: (0, j))]
    args = [a, w]
    body = _mm_kernel
    if scale is not None:
        in_specs.append(pl.BlockSpec((1, tn), lambda i, j: (0, j)))
        args.append(scale)
        body = _mm_scale_kernel
    return pl.pallas_call(
        body, name="mm", grid=(m // tm, n // tn), in_specs=in_specs,
        out_specs=pl.BlockSpec((tm, tn), lambda i, j: (i, j)),
        out_shape=jax.ShapeDtypeStruct((m, n), out_dtype),
        compiler_params=_cparams(("parallel", "arbitrary")))(*args)


def _matmul_swiglu(a, w):
    m, k = a.shape
    n = w.shape[1] // 2
    tm = min(MM_TM, m)
    tn = min(MM_TN // 2, n)
    nj = n // tn
    return pl.pallas_call(
        _mm_swiglu_kernel, name="mm_swiglu", grid=(m // tm, nj),
        in_specs=[pl.BlockSpec((tm, k), lambda i, j: (i, 0)),
                  pl.BlockSpec((k, tn), lambda i, j: (0, j)),
                  pl.BlockSpec((k, tn), lambda i, j: (0, j + nj))],
        out_specs=pl.BlockSpec((tm, tn), lambda i, j: (i, j)),
        out_shape=jax.ShapeDtypeStruct((m, n), BF16),
        compiler_params=_cparams(("parallel", "arbitrary")))(a, w, w)


def _matmul_t(w_t, a):
    m, k = a.shape
    n = w_t.shape[0]
    tm = min(MM_TM, m)
    return pl.pallas_call(
        _mm_t_kernel, name="mm_dt_t", grid=(m // tm,),
        in_specs=[pl.BlockSpec((n, k), lambda i: (0, 0)), pl.BlockSpec((tm, k), lambda i: (i, 0))],
        out_specs=pl.BlockSpec((n, tm), lambda i: (0, i)),
        out_shape=jax.ShapeDtypeStruct((n, m), F32),
        compiler_params=_cparams(("parallel",)))(w_t, a)


def _rmsnorm_kernel(x_ref, g_ref, o_ref):
    o_ref[...] = (_rms(x_ref[...], NORM_EPS) * g_ref[...]).astype(o_ref.dtype)


def _rmsnorm(x, g):
    t, d = x.shape
    tr = min(ROW_TILE, t)
    return pl.pallas_call(
        _rmsnorm_kernel, name="rmsnorm", grid=(t // tr,),
        in_specs=[pl.BlockSpec((tr, d), lambda i: (i, 0)), pl.BlockSpec((1, d), lambda i: (0, 0))],
        out_specs=pl.BlockSpec((tr, d), lambda i: (i, 0)),
        out_shape=jax.ShapeDtypeStruct((t, d), BF16),
        compiler_params=_cparams(("parallel",)))(x, g.reshape(1, d))


def _resid_norm_kernel(x_ref, y_ref, g1_ref, g2_ref, xo_ref, h_ref, *, scale):
    xn = x_ref[...] + scale * (_rms(y_ref[...], NORM_EPS) * g1_ref[...])
    xo_ref[...] = xn
    h_ref[...] = (_rms(xn, NORM_EPS) * g2_ref[...]).astype(h_ref.dtype)


def _resid_kernel(x_ref, y_ref, g1_ref, xo_ref, *, scale):
    xo_ref[...] = x_ref[...] + scale * (_rms(y_ref[...], NORM_EPS) * g1_ref[...])


def _resid_norm(x, y, g_post, scale, g_next):
    t, d = x.shape
    tr = min(ROW_TILE, t)
    row = pl.BlockSpec((tr, d), lambda i: (i, 0))
    vec = pl.BlockSpec((1, d), lambda i: (0, 0))
    if g_next is None:
        return pl.pallas_call(
            functools.partial(_resid_kernel, scale=scale), name="resid", grid=(t // tr,),
            in_specs=[row, row, vec], out_specs=row,
            out_shape=jax.ShapeDtypeStruct((t, d), F32),
            compiler_params=_cparams(("parallel",)))(x, y, g_post.reshape(1, d)), None
    return pl.pallas_call(
        functools.partial(_resid_norm_kernel, scale=scale), name="resid_norm", grid=(t // tr,),
        in_specs=[row, row, vec, vec], out_specs=[row, row],
        out_shape=[jax.ShapeDtypeStruct((t, d), F32), jax.ShapeDtypeStruct((t, d), BF16)],
        compiler_params=_cparams(("parallel",)))(x, y, g_post.reshape(1, d), g_next.reshape(1, d))


def _conv_kernel(prev_ref, x_ref, next_ref, w_ref, b_ref, o_ref, *, rows, width, start_tiles, end_tiles):
    i = pl.program_id(0)
    halo = BF16_SUBLANES
    pad = width // 2
    is_start = functools.reduce(jnp.logical_or, [i == s for s in start_tiles])
    is_end = functools.reduce(jnp.logical_or, [i == e for e in end_tiles])
    prev = jnp.where(is_start, 0.0, prev_ref[...].astype(F32))
    nxt = jnp.where(is_end, 0.0, next_ref[...].astype(F32))
    ext = jnp.concatenate([prev, x_ref[...].astype(F32), nxt], axis=0)
    w = w_ref[...]
    acc = jnp.zeros(x_ref.shape, F32) + b_ref[...]
    for k in range(width):
        off = halo - pad + k
        acc = acc + w[k:k + 1, :] * ext[off:off + rows, :]
    o_ref[...] = _silu(acc).astype(o_ref.dtype)


def _conv_silu(proj, col_off, conv_w, conv_b, seq_lens):
    t = proj.shape[0]
    width, ch = conv_w.shape
    rows = min(CONV_ROWS, min(seq_lens))
    cols = min(CONV_COLS, ch)
    halo = BF16_SUBLANES
    assert col_off % cols == 0 and ch % cols == 0 and all(s % rows == 0 for s in seq_lens)
    starts, ends, pos = [], [], 0
    for s in seq_lens:
        starts.append(pos // rows)
        pos += s
        ends.append(pos // rows - 1)
    rb = rows // halo
    nhalo = t // halo
    cb = col_off // cols
    return pl.pallas_call(
        functools.partial(_conv_kernel, rows=rows, width=width, start_tiles=tuple(starts), end_tiles=tuple(ends)),
        name="conv_silu", grid=(t // rows, ch // cols),
        in_specs=[pl.BlockSpec((halo, cols), lambda i, j: (jnp.maximum(i * rb - 1, 0), cb + j)),
                  pl.BlockSpec((rows, cols), lambda i, j: (i, cb + j)),
                  pl.BlockSpec((halo, cols), lambda i, j: (jnp.minimum((i + 1) * rb, nhalo - 1), cb + j)),
                  pl.BlockSpec((width, cols), lambda i, j: (0, j)),
                  pl.BlockSpec((1, cols), lambda i, j: (0, j))],
        out_specs=pl.BlockSpec((rows, cols), lambda i, j: (i, j)),
        out_shape=jax.ShapeDtypeStruct((t, ch), BF16),
        compiler_params=_cparams(("parallel", "arbitrary")))(proj, proj, proj, conv_w, conv_b.reshape(1, ch))


def _ssd_kernel(*refs, reverse, n_chunks, reset_chunks, groups, heads, d_inner):
    if reverse:
        (xbc_ref, dt_ref, dtt_ref, dtb_ref, dtbt_ref, alog_ref, alogt_ref, tri_ref, trit_ref, exp_ref,
         yf_ref, z_ref, nw_ref, o_ref, state_ref) = refs
    else:
        (xbc_ref, dt_ref, dtt_ref, dtb_ref, dtbt_ref, alog_ref, alogt_ref, tri_ref, trit_ref, exp_ref,
         dskip_ref, o_ref, state_ref) = refs
    step = pl.program_id(0)
    chunk = (n_chunks - 1 - step) if reverse else step
    L = SSD_CHUNK
    gw = d_inner // groups
    hpg = heads // groups
    n = SSM_STATE
    col0 = heads if reverse else 0

    @pl.when(functools.reduce(jnp.logical_or, [chunk == c for c in reset_chunks]))
    def _():
        state_ref[...] = jnp.zeros_like(state_ref)

    dt = _softplus(dt_ref[...] + dtb_ref[...])
    dta = dt * (-jnp.exp(alog_ref[...]))
    dt_t = _softplus(dtt_ref[...] + dtbt_ref[...])
    dta_t = dt_t * (-jnp.exp(alogt_ref[...]))
    tri = tri_ref[...]
    tri_t = trit_ref[...]
    acum = sum(_dot(tri, p) for p in _split_bf16(dta, 3))
    acum_t = sum(_dot(p, tri_t) for p in _split_bf16(dta_t, 3))
    edge = 0 if reverse else L - 1
    a_end = acum[edge:edge + 1, :]
    to_end = jnp.exp(a_end - acum) * dt
    ea = jnp.exp(acum)
    expand = exp_ref[...]
    to_end_x = sum(_dot(p, expand) for p in _split_bf16(to_end, 2))
    ea_x = sum(_dot(p, expand) for p in _split_bf16(ea, 2))
    decay_x = ea_x[edge:edge + 1, :]

    row = lax.broadcasted_iota(jnp.int32, (L, L), 0)
    colm = lax.broadcasted_iota(jnp.int32, (L, L), 1)
    causal = (row <= colm) if reverse else (row >= colm)
    lane = lax.broadcasted_iota(jnp.int32, (L, LANES), 1)
    first_half = lane < SSM_HEAD_DIM

    for g in range(groups):
        xs = xbc_ref[:, g * gw:(g + 1) * gw]
        bm = xbc_ref[:, d_inner + g * n:d_inner + (g + 1) * n]
        cm = xbc_ref[:, d_inner + groups * n + g * n:d_inner + groups * n + (g + 1) * n]
        cb = _dot_nt(cm, bm)
        st = state_ref[g]
        y_off = _dot(cm, st.astype(BF16)) * ea_x[:, g * gw:(g + 1) * gw]
        xs_scaled = (xs.astype(F32) * to_end_x[:, g * gw:(g + 1) * gw]).astype(BF16)
        state_ref[g] = st * decay_x[:, g * gw:(g + 1) * gw] + _dot_tn(bm, xs_scaled)
        pairs = []
        for k in range(hpg // 2):
            xp = xs[:, k * LANES:(k + 1) * LANES]
            y_pair = None
            for side in range(2):
                c = col0 + g * hpg + 2 * k + side
                seg = acum[:, c:c + 1] - acum_t[c:c + 1, :]
                decay = jnp.exp(jnp.where(causal, seg, -jnp.inf))
                m = (cb * decay * dt_t[c:c + 1, :]).astype(BF16)
                keep = first_half if side == 0 else jnp.logical_not(first_half)
                part = _dot(m, jnp.where(keep, xp, jnp.zeros_like(xp)))
                y_pair = part if y_pair is None else y_pair + part
            lo = g * gw + k * LANES
            y_pair = y_pair + y_off[:, k * LANES:(k + 1) * LANES]
            if not reverse:
                o_ref[:, lo:lo + LANES] = y_pair + dskip_ref[:, lo:lo + LANES] * xp.astype(F32)
            else:
                pairs.append(y_pair)
        if reverse:
            y = jnp.concatenate(pairs, axis=1) + yf_ref[:, g * gw:(g + 1) * gw]
            y = y * _silu(z_ref[:, g * gw:(g + 1) * gw].astype(F32))
            o_ref[:, g * gw:(g + 1) * gw] = (_rms(y, NORM_EPS) * nw_ref[:, g * gw:(g + 1) * gw]).astype(o_ref.dtype)


def _ssd(xbc, proj, dt_raw, dt_raw_t, dt_bias, a_log, d_skip, norm_w, seq_lens, groups):
    t = xbc.shape[0]
    heads = d_skip.shape[0]
    d_inner = heads * SSM_HEAD_DIM
    L = SSD_CHUNK
    nc = t // L
    assert 2 * heads <= LANES and (heads // groups) % 2 == 0
    starts, lasts, pos = [], [], 0
    for s in seq_lens:
        assert s % L == 0
        starts.append(pos // L)
        pos += s
        lasts.append(pos // L - 1)
    padh = LANES - 2 * heads
    dtb = jnp.pad(dt_bias.reshape(-1), (0, padh))
    alog = jnp.pad(a_log.reshape(-1), (0, padh))
    tri_lo = jnp.tril(jnp.ones((L, L), F32)).astype(BF16)
    tri_up = jnp.triu(jnp.ones((L, L), F32)).astype(BF16)
    head_of_col = jnp.arange(d_inner) // SSM_HEAD_DIM
    dskip_x = jnp.repeat(d_skip.astype(F32), SSM_HEAD_DIM).reshape(1, d_inner)

    def run(reverse, extra_args, extra_specs, out_dtype):
        cidx = (lambda i: nc - 1 - i) if reverse else (lambda i: i)
        tri = tri_up if reverse else tri_lo
        expand = (jnp.arange(LANES)[:, None] == (head_of_col + (heads if reverse else 0))[None, :]).astype(BF16)
        const = lambda shape: pl.BlockSpec(shape, lambda i: (0, 0))
        in_specs = [pl.BlockSpec((L, xbc.shape[1]), lambda i: (cidx(i), 0)),
                    pl.BlockSpec((L, LANES), lambda i: (cidx(i), 0)),
                    pl.BlockSpec((LANES, L), lambda i: (0, cidx(i))),
                    const((1, LANES)), const((LANES, 1)), const((1, LANES)), const((LANES, 1)),
                    const((L, L)), const((L, L)), const((LANES, d_inner))] + extra_specs(cidx)
        return pl.pallas_call(
            functools.partial(_ssd_kernel, reverse=reverse, n_chunks=nc,
                              reset_chunks=tuple(lasts if reverse else starts),
                              groups=groups, heads=heads, d_inner=d_inner),
            name="ssd_bwd" if reverse else "ssd_fwd", grid=(nc,), in_specs=in_specs,
            out_specs=pl.BlockSpec((L, d_inner), lambda i: (cidx(i), 0)),
            out_shape=jax.ShapeDtypeStruct((t, d_inner), out_dtype),
            scratch_shapes=[pltpu.VMEM((groups, SSM_STATE, d_inner // groups), F32)],
            compiler_params=_cparams(("arbitrary",)))(
                xbc, dt_raw, dt_raw_t, dtb.reshape(1, LANES), dtb.reshape(LANES, 1), alog.reshape(1, LANES),
                alog.reshape(LANES, 1), tri, tri.T, expand, *extra_args)

    y_fwd = run(False, [dskip_x], lambda cidx: [pl.BlockSpec((1, d_inner), lambda i: (0, 0))], F32)
    return run(True, [y_fwd, proj, norm_w.reshape(1, d_inner)],
               lambda cidx: [pl.BlockSpec((L, d_inner), lambda i: (cidx(i), 0)),
                             pl.BlockSpec((L, d_inner), lambda i: (cidx(i), 0)),
                             pl.BlockSpec((1, d_inner), lambda i: (0, 0))], BF16)


def _t5_bucket(rel):
    half = NUM_BUCKETS // 2
    max_exact = half // 2
    n = jnp.abs(rel)
    log_ratio = jnp.log(jnp.maximum(n, 1).astype(F32) / max_exact) / math.log(MAX_DISTANCE / max_exact)
    large = jnp.minimum(max_exact + (log_ratio * (half - max_exact)).astype(jnp.int32), half - 1)
    return jnp.where(rel > 0, half, 0) + jnp.where(n < max_exact, n, large)


def _attn_kernel(far_ref, lq_ref, sw_ref, q_ref, k_ref, v_ref, bias_ref, o_ref, m_ref, l_ref, acc_ref,
                 *, t, nk, lambda_init):
    h = pl.program_id(1)
    i = pl.program_id(2)
    dh = DIFF_HEAD_DIM
    c_before = far_ref[h, 0]
    c_after = far_ref[h, 1]
    m_ref[...] = jnp.full(m_ref.shape, -jnp.inf, F32)
    l_ref[...] = jnp.zeros(l_ref.shape, F32)
    acc_ref[...] = jnp.zeros(acc_ref.shape, F32)

    def update(kb, bias_tile, bias_const):
        ks = k_ref[pl.ds(pl.multiple_of(kb * t, t), t), :]
        vs = v_ref[pl.ds(pl.multiple_of(kb * t, t), t), :]
        for j in range(2):
            s = _dot_nt(q_ref[:, j * dh:(j + 1) * dh], ks[:, j * dh:(j + 1) * dh])
            if bias_tile is not None:
                s = s + bias_tile
                shift = 0.0
            else:
                shift = bias_const
            m_old = m_ref[j]
            m_new = jnp.maximum(m_old, jnp.max(s, axis=-1, keepdims=True) + shift)
            p = jnp.exp(s - (m_new - shift))
            alpha = jnp.exp(m_old - m_new)
            l_ref[j] = alpha * l_ref[j] + jnp.sum(p, axis=-1, keepdims=True)
            acc_ref[j] = alpha * acc_ref[j] + _dot(p.astype(BF16), vs)
            m_ref[j] = m_new

    def far_before(kb, carry):
        update(kb, None, c_before)
        return carry

    def far_after(kb, carry):
        update(kb, None, c_after)
        return carry

    lax.fori_loop(0, jnp.maximum(i - 1, 0), far_before, 0)

    @pl.when(i >= 1)
    def _():
        update(i - 1, bias_ref[0, 0], None)

    update(i, bias_ref[0, 1], None)

    @pl.when(i + 1 < nk)
    def _():
        update(i + 1, bias_ref[0, 2], None)

    lax.fori_loop(jnp.minimum(i + 2, nk), nk, far_after, 0)

    lq = lq_ref[...]
    lam = (jnp.exp(jnp.sum(lq[0:1] * lq[1:2], axis=-1, keepdims=True))
           - jnp.exp(jnp.sum(lq[2:3] * lq[3:4], axis=-1, keepdims=True)) + lambda_init)
    o = acc_ref[0] / l_ref[0] - lam * (acc_ref[1] / l_ref[1])
    o_ref[...] = (_rms(o, SUBLN_EPS) * sw_ref[...] * (1.0 - lambda_init)).astype(o_ref.dtype)


def _diff_attention(proj, q_off, n_heads, rel_bias, lambda_qk, subln_w, lambda_init, tok_off, bsz, seq):
    t = min(ATTN_TILE, seq)
    nk = seq // t
    hw = 2 * DIFF_HEAD_DIM
    assert t >= MAX_DISTANCE and seq % t == 0 and tok_off % seq == 0 and q_off % hw == 0
    qb = q_off // hw
    ii = jnp.arange(t)
    rel = (jnp.arange(-1, 2) * t)[:, None, None] + ii[None, None, :] - ii[None, :, None]
    table_t = rel_bias.astype(F32).T
    bias_near = table_t[:, _t5_bucket(rel)]
    far = table_t[:, _t5_bucket(jnp.array([-2 * MAX_DISTANCE, 2 * MAX_DISTANCE]))]
    row0 = tok_off // t
    seq0 = tok_off // seq
    return pl.pallas_call(
        functools.partial(_attn_kernel, t=t, nk=nk, lambda_init=lambda_init),
        name="diff_attn", grid=(bsz, n_heads, nk),
        in_specs=[pl.BlockSpec(memory_space=pltpu.SMEM),
                  pl.BlockSpec((4, DIFF_HEAD_DIM), lambda b, h, i: (0, 0)),
                  pl.BlockSpec((1, hw), lambda b, h, i: (0, 0)),
                  pl.BlockSpec((t, hw), lambda b, h, i: (row0 + b * nk + i, qb + h)),
                  pl.BlockSpec((seq, hw), lambda b, h, i: (seq0 + b, qb + n_heads + h)),
                  pl.BlockSpec((seq, hw), lambda b, h, i: (seq0 + b, qb + 2 * n_heads + h)),
                  pl.BlockSpec((1, 3, t, t), lambda b, h, i: (h, 0, 0, 0))],
        out_specs=pl.BlockSpec((t, hw), lambda b, h, i: (b * nk + i, h)),
        out_shape=jax.ShapeDtypeStruct((bsz * seq, n_heads * hw), BF16),
        scratch_shapes=[pltpu.VMEM((2, t, 1), F32), pltpu.VMEM((2, t, 1), F32), pltpu.VMEM((2, t, hw), F32)],
        compiler_params=_cparams(("parallel", "parallel", "arbitrary")))(
            far, lambda_qk.astype(F32), subln_w.reshape(1, hw).astype(F32), proj, proj, proj, bias_near)


def _gelu_tanh(x):
    return 0.5 * x * (1.0 + jnp.tanh(math.sqrt(2.0 / math.pi) * (x + 0.044715 * (x * x * x))))


def _gmlp_kernel(u_ref, v_ref, lng_ref, lnb_ref, ws_ref, bs_ref, o_ref, *, rows, groups):
    u = _gelu_tanh(u_ref[...].astype(F32))
    v = _gelu_tanh(v_ref[...].astype(F32))
    mu = jnp.mean(v, axis=-1, keepdims=True)
    vc = v - mu
    vn = (vc * lax.rsqrt(jnp.mean(vc * vc, axis=-1, keepdims=True) + LN_EPS) * lng_ref[...] + lnb_ref[...]).astype(BF16)
    gd = v.shape[1] // groups
    L = GMLP_CHUNK
    for c in range(rows // L):
        for g in range(groups):
            mixed = _dot(ws_ref[g], vn[c * L:(c + 1) * L, g * gd:(g + 1) * gd]) + bs_ref[:, g * gd:(g + 1) * gd]
            o_ref[c * L:(c + 1) * L, g * gd:(g + 1) * gd] = (
                u[c * L:(c + 1) * L, g * gd:(g + 1) * gd] * mixed).astype(o_ref.dtype)


def _gmlp(proj, u_off, width, ln_v, w_s, b_s):
    t = proj.shape[0]
    groups = w_s.shape[0]
    L = GMLP_CHUNK
    rows = min(GMLP_ROWS, t)
    assert u_off % width == 0 and rows % L == 0
    ub = u_off // width
    bias_x = jnp.repeat(b_s.astype(F32).T, width // groups, axis=1)
    vec = pl.BlockSpec((1, width), lambda i: (0, 0))
    return pl.pallas_call(
        functools.partial(_gmlp_kernel, rows=rows, groups=groups), name="gmlp", grid=(t // rows,),
        in_specs=[pl.BlockSpec((rows, width), lambda i: (i, ub)),
                  pl.BlockSpec((rows, width), lambda i: (i, ub + 1)),
                  vec, vec,
                  pl.BlockSpec((groups, L, L), lambda i: (0, 0, 0)),
                  pl.BlockSpec((L, width), lambda i: (0, 0))],
        out_specs=pl.BlockSpec((rows, width), lambda i: (i, 0)),
        out_shape=jax.ShapeDtypeStruct((t, width), BF16),
        compiler_params=_cparams(("parallel",)))(
            proj, proj, ln_v[0].reshape(1, width).astype(F32), ln_v[1].reshape(1, width).astype(F32),
            w_s.astype(BF16), bias_x)


def _merge_kernel(h_ref, s_ref, a_ref, m_ref, wg0_ref, wg1_ref, wg2_ref, wos_ref, woa_ref, wom_ref, o_ref):
    h = h_ref[...]
    merged = (jax.nn.sigmoid(_dot(h, wg0_ref[...])) * _dot(s_ref[...], wos_ref[...])
              + jax.nn.sigmoid(_dot(h, wg1_ref[...])) * _dot(a_ref[...], woa_ref[...])
              + jax.nn.sigmoid(_dot(h, wg2_ref[...])) * _dot(m_ref[...], wom_ref[...]))
    o_ref[...] = merged.astype(o_ref.dtype)


def _merge(h, y_s, y_a, y_m, w_gate, w_os, w_oa, w_om):
    t, d = h.shape
    tm = min(MERGE_TM, t)
    tn = min(MERGE_TN, d)
    nj = d // tn
    act = lambda a: pl.BlockSpec((tm, a.shape[1]), lambda i, j: (i, 0))
    wcol = lambda w, off: pl.BlockSpec((w.shape[0], tn), lambda i, j: (0, j + off))
    return pl.pallas_call(
        _merge_kernel, name="merge", grid=(t // tm, nj),
        in_specs=[act(h), act(y_s), act(y_a), act(y_m),
                  wcol(w_gate, 0), wcol(w_gate, nj), wcol(w_gate, 2 * nj),
                  wcol(w_os, 0), wcol(w_oa, 0), wcol(w_om, 0)],
        out_specs=pl.BlockSpec((tm, tn), lambda i, j: (i, j)),
        out_shape=jax.ShapeDtypeStruct((t, d), BF16),
        compiler_params=_cparams(("parallel", "arbitrary")))(
            h, y_s, y_a, y_m, w_gate, w_gate, w_gate, w_os, w_oa, w_om)


def kernel(x_prompt, x_sample, rel_bias, norms, w_ffn1_in, w_ffn1_out, w_in, conv_w, conv_b, dt_bias, a_log,
           d_skip, ssm_norm, w_o_ssm, lambda_qk, diff_subln, w_o_diff, ln_v, w_spatial, b_spatial, w_o_gmlp,
           w_out, w_ffn2_in, w_ffn2_out):
    depth = norms.shape[0]
    d = x_prompt.shape[-1]
    d_inner = w_o_ssm.shape[1]
    conv_ch = conv_w.shape[2]
    heads = d_skip.shape[1]
    groups = (conv_ch - d_inner) // (2 * SSM_STATE)
    diff_w = w_o_diff.shape[1]
    diff_heads = diff_w // (2 * DIFF_HEAD_DIM)
    gmlp_w = w_o_gmlp.shape[1]
    pb, ps = x_prompt.shape[:2]
    sb, ss = x_sample.shape[:2]
    seq_lens = [ps] * pb + [ss] * sb
    n_prompt = pb * ps

    c_dt = d_inner + conv_ch
    c_diff = c_dt + 2 * heads
    c_gmlp = c_diff + 3 * diff_w
    c_gate = c_gmlp + 2 * gmlp_w
    q_off = c_dt
    u_off = q_off + 3 * diff_w
    n_main = u_off + 2 * gmlp_w
    col_scale = jnp.ones((1, n_main), F32).at[:, q_off:q_off + diff_w].set(DIFF_HEAD_DIM ** -0.5)

    x = jnp.concatenate([x_prompt.reshape(n_prompt, d), x_sample.reshape(sb * ss, d)], axis=0)
    h = _rmsnorm(x, norms[0, 0])
    for l in range(depth):
        n = norms[l]
        lambda_init = 0.8 - 0.6 * math.exp(-0.3 * l)
        wi = w_in[l]
        w_main = jnp.concatenate([wi[:, :c_dt], wi[:, c_diff:c_gate]], axis=1).astype(BF16)
        w_dt = jnp.pad(wi[:, c_dt:c_diff], ((0, 0), (0, LANES - 2 * heads))).astype(BF16)
        w_gate = wi[:, c_gate:].astype(BF16)

        y = _matmul(_matmul_swiglu(h, w_ffn1_in[l].astype(BF16)), w_ffn1_out[l].astype(BF16), F32)
        x, h = _resid_norm(x, y, n[1], 0.5, n[2])

        proj = _matmul(h, w_main, BF16, scale=col_scale)
        dt_raw = _matmul(h, w_dt, F32, tn=LANES)
        dt_raw_t = _matmul_t(w_dt.T, h)
        xbc = _conv_silu(proj, d_inner, conv_w[l], conv_b[l], seq_lens)
        y_ssm = _ssd(xbc, proj, dt_raw, dt_raw_t, dt_bias[l], a_log[l], d_skip[l], ssm_norm[l], seq_lens, groups)
        y_att = jnp.concatenate([
            _diff_attention(proj, q_off, diff_heads, rel_bias, lambda_qk[l], diff_subln[l], lambda_init, 0, pb, ps),
            _diff_attention(proj, q_off, diff_heads, rel_bias, lambda_qk[l], diff_subln[l], lambda_init,
                            n_prompt, sb, ss)], axis=0)
        y_gmlp = _gmlp(proj, u_off, gmlp_w, ln_v[l], w_spatial[l], b_spatial[l])
        merged = _merge(h, y_ssm, y_att, y_gmlp, w_gate, w_o_ssm[l].astype(BF16), w_o_diff[l].astype(BF16),
                        w_o_gmlp[l].astype(BF16))
        y = _matmul(merged, w_out[l].astype(BF16), F32)
        x, h = _resid_norm(x, y, n[3], 1.0, n[4])

        y = _matmul(_matmul_swiglu(h, w_ffn2_in[l].astype(BF16)), w_ffn2_out[l].astype(BF16), F32)
        x, h = _resid_norm(x, y, n[5], 0.5, norms[l + 1, 0] if l + 1 < depth else None)

    return x[:n_prompt].reshape(x_prompt.shape), x[n_prompt:].reshape(x_sample.shape)
```

```python
import functools
import math

import jax
import jax.numpy as jnp
from jax import lax
from jax.experimental import pallas as pl
from jax.experimental.pallas import tpu as pltpu

F32 = jnp.float32
BF16 = jnp.bfloat16

SSM_HEAD_DIM = 64
SSM_STATE = 128
SSD_CHUNK = 128
DIFF_HEAD_DIM = 128
NUM_BUCKETS = 32
MAX_DISTANCE = 128
GMLP_CHUNK = 128
N_BRANCHES = 3
NORM_EPS = 1e-6
SUBLN_EPS = 1e-5
LN_EPS = 1e-5
LOG2E = math.log2(math.e)

LANES = 128
BF16_SUBLANES = 16
VMEM_LIMIT_BYTES = 56 * 1024 * 1024

MM_TM = 1024
MM_TN = 1024
ROW_TILE = 256
CONV_ROWS = 512
CONV_COLS = 512
ATTN_TILE = 512
MERGE_TM = 512
MERGE_TN = 256
GMLP_ROWS = 256


def _cparams(sem):
    return pltpu.CompilerParams(dimension_semantics=sem, vmem_limit_bytes=VMEM_LIMIT_BYTES)


def _dot(a, b):
    return jnp.dot(a, b, preferred_element_type=F32)


def _dot_nt(a, b):
    return lax.dot_general(a, b, (((1,), (1,)), ((), ())), preferred_element_type=F32)


def _dot_tn(a, b):
    return lax.dot_general(a, b, (((0,), (0,)), ((), ())), preferred_element_type=F32)


def _split_bf16(x, n):
    parts = []
    r = x
    for _ in range(n):
        p = r.astype(BF16)
        parts.append(p)
        r = r - p.astype(F32)
    return parts


def _rms(x, eps):
    return x * lax.rsqrt(jnp.mean(x * x, axis=-1, keepdims=True) + eps)


def _silu(x):
    return x * jax.nn.sigmoid(x)


def _softplus(x):
    return jnp.maximum(x, 0.0) + jnp.log1p(jnp.exp(-jnp.abs(x)))


def _mm_scale_kernel(a_ref, w_ref, s_ref, o_ref):
    o_ref[...] = (_dot(a_ref[...], w_ref[...]) * s_ref[...]).astype(o_ref.dtype)


def _mm_kernel(a_ref, w_ref, o_ref):
    o_ref[...] = _dot(a_ref[...], w_ref[...]).astype(o_ref.dtype)


def _mm_swiglu_kernel(a_ref, wg_ref, wu_ref, o_ref):
    a = a_ref[...]
    g = _dot(a, wg_ref[...])
    u = _dot(a, wu_ref[...])
    o_ref[...] = (_silu(g) * u).astype(o_ref.dtype)


def _mm_t_kernel(w_ref, a_ref, o_ref):
    o_ref[...] = _dot_nt(w_ref[...], a_ref[...]).astype(o_ref.dtype)


def _matmul(a, w, out_dtype, scale=None, tm=None, tn=None):
    m, k = a.shape
    n = w.shape[1]
    tm = min(tm or MM_TM, m)
    tn = min(tn or MM_TN, n)
    in_specs = [pl.BlockSpec((tm, k), lambda i, j: (i, 0)), pl.BlockSpec((k, tn), lambda i, j: (0, j))]
    args = [a, w]
    body = _mm_kernel
    if scale is not None:
        in_specs.append(pl.BlockSpec((1, tn), lambda i, j: (0, j)))
        args.append(scale)
        body = _mm_scale_kernel
    return pl.pallas_call(
        body, name="mm", grid=(m // tm, n // tn), in_specs=in_specs,
        out_specs=pl.BlockSpec((tm, tn), lambda i, j: (i, j)),
        out_shape=jax.ShapeDtypeStruct((m, n), out_dtype),
        compiler_params=_cparams(("parallel", "arbitrary")))(*args)


def _matmul_swiglu(a, w):
    m, k = a.shape
    n = w.shape[1] // 2
    tm = min(MM_TM, m)
    tn = min(MM_TN // 2, n)
    nj = n // tn
    return pl.pallas_call(
        _mm_swiglu_kernel, name="mm_swiglu", grid=(m // tm, nj),
        in_specs=[pl.BlockSpec((tm, k), lambda i, j: (i, 0)),
                  pl.BlockSpec((k, tn), lambda i, j: (0, j)),
                  pl.BlockSpec((k, tn), lambda i, j: (0, j + nj))],
        out_specs=pl.BlockSpec((tm, tn), lambda i, j: (i, j)),
        out_shape=jax.ShapeDtypeStruct((m, n), BF16),
        compiler_params=_cparams(("parallel", "arbitrary")))(a, w, w)


def _matmul_t(w_t, a):
    m, k = a.shape
    n = w_t.shape[0]
    tm = min(MM_TM, m)
    return pl.pallas_call(
        _mm_t_kernel, name="mm_dt_t", grid=(m // tm,),
        in_specs=[pl.BlockSpec((n, k), lambda i: (0, 0)), pl.BlockSpec((tm, k), lambda i: (i, 0))],
        out_specs=pl.BlockSpec((n, tm), lambda i: (0, i)),
        out_shape=jax.ShapeDtypeStruct((n, m), F32),
        compiler_params=_cparams(("parallel",)))(w_t, a)


def _stream_specs(parts, tr):
    d = parts[0].shape[1]
    if len(parts) == 1:
        return [pl.BlockSpec((tr, d), lambda i: (i, 0))], None
    na = parts[0].shape[0] // tr
    return [pl.BlockSpec((tr, d), lambda i: (jnp.minimum(i, na - 1), 0)),
            pl.BlockSpec((tr, d), lambda i: (jnp.maximum(i - na, 0), 0))], na


def _stream_block(x_refs, first_blocks):
    if len(x_refs) == 1:
        return x_refs[0][...]
    return jnp.where(pl.program_id(0) < first_blocks, x_refs[0][...], x_refs[1][...])


def _rmsnorm_kernel(*refs, n_parts, first_blocks):
    g_ref, o_ref = refs[n_parts:]
    o_ref[...] = (_rms(_stream_block(refs[:n_parts], first_blocks), NORM_EPS) * g_ref[...]).astype(o_ref.dtype)


def _rmsnorm(parts, g):
    d = parts[0].shape[1]
    t = sum(p.shape[0] for p in parts)
    tr = min(ROW_TILE, min(p.shape[0] for p in parts))
    x_specs, first_blocks = _stream_specs(parts, tr)
    return pl.pallas_call(
        functools.partial(_rmsnorm_kernel, n_parts=len(parts), first_blocks=first_blocks),
        name="rmsnorm", grid=(t // tr,),
        in_specs=x_specs + [pl.BlockSpec((1, d), lambda i: (0, 0))],
        out_specs=pl.BlockSpec((tr, d), lambda i: (i, 0)),
        out_shape=jax.ShapeDtypeStruct((t, d), BF16),
        compiler_params=_cparams(("parallel",)))(*parts, g.reshape(1, d))


def _resid_norm_kernel(*refs, n_parts, first_blocks, scale):
    y_ref, g1_ref, g2_ref, xo_ref, h_ref = refs[n_parts:]
    xn = _stream_block(refs[:n_parts], first_blocks) + scale * (_rms(y_ref[...], NORM_EPS) * g1_ref[...])
    xo_ref[...] = xn
    h_ref[...] = (_rms(xn, NORM_EPS) * g2_ref[...]).astype(h_ref.dtype)


def _resid_kernel(x_ref, y_ref, g1_ref, xo_ref, *, scale):
    xo_ref[...] = x_ref[...] + scale * (_rms(y_ref[...], NORM_EPS) * g1_ref[...])


def _resid_norm(parts, y, g_post, scale, g_next):
    t, d = y.shape
    tr = min(ROW_TILE, min(p.shape[0] for p in parts))
    x_specs, first_blocks = _stream_specs(parts, tr)
    row = pl.BlockSpec((tr, d), lambda i: (i, 0))
    vec = pl.BlockSpec((1, d), lambda i: (0, 0))
    return pl.pallas_call(
        functools.partial(_resid_norm_kernel, n_parts=len(parts), first_blocks=first_blocks, scale=scale),
        name="resid_norm", grid=(t // tr,),
        in_specs=x_specs + [row, vec, vec], out_specs=[row, row],
        out_shape=[jax.ShapeDtypeStruct((t, d), F32), jax.ShapeDtypeStruct((t, d), BF16)],
        compiler_params=_cparams(("parallel",)))(*parts, y, g_post.reshape(1, d), g_next.reshape(1, d))


def _resid_rows(x, y, g_post, scale, row_off, n_rows):
    d = x.shape[1]
    tr = min(ROW_TILE, n_rows)
    assert row_off % tr == 0 and n_rows % tr == 0
    b0 = row_off // tr
    row = pl.BlockSpec((tr, d), lambda i: (b0 + i, 0))
    return pl.pallas_call(
        functools.partial(_resid_kernel, scale=scale), name="resid", grid=(n_rows // tr,),
        in_specs=[row, row, pl.BlockSpec((1, d), lambda i: (0, 0))],
        out_specs=pl.BlockSpec((tr, d), lambda i: (i, 0)),
        out_shape=jax.ShapeDtypeStruct((n_rows, d), F32),
        compiler_params=_cparams(("parallel",)))(x, y, g_post.reshape(1, d))


def _conv_kernel(prev_ref, x_ref, next_ref, w_ref, b_ref, o_ref, *, rows, width, start_tiles, end_tiles):
    i = pl.program_id(0)
    halo = BF16_SUBLANES
    pad = width // 2
    is_start = functools.reduce(jnp.logical_or, [i == s for s in start_tiles])
    is_end = functools.reduce(jnp.logical_or, [i == e for e in end_tiles])
    prev = jnp.where(is_start, 0.0, prev_ref[...].astype(F32))
    nxt = jnp.where(is_end, 0.0, next_ref[...].astype(F32))
    ext = jnp.concatenate([prev, x_ref[...].astype(F32), nxt], axis=0)
    w = w_ref[...]
    acc = jnp.zeros(x_ref.shape, F32) + b_ref[...]
    for k in range(width):
        off = halo - pad + k
        acc = acc + w[k:k + 1, :] * ext[off:off + rows, :]
    o_ref[...] = _silu(acc).astype(o_ref.dtype)


def _conv_silu(proj, col_off, conv_w, conv_b, seq_lens):
    t = proj.shape[0]
    width, ch = conv_w.shape
    rows = min(CONV_ROWS, min(seq_lens))
    cols = min(CONV_COLS, ch)
    halo = BF16_SUBLANES
    assert col_off % cols == 0 and ch % cols == 0 and all(s % rows == 0 for s in seq_lens)
    starts, ends, pos = [], [], 0
    for s in seq_lens:
        starts.append(pos // rows)
        pos += s
        ends.append(pos // rows - 1)
    rb = rows // halo
    nhalo = t // halo
    cb = col_off // cols
    return pl.pallas_call(
        functools.partial(_conv_kernel, rows=rows, width=width, start_tiles=tuple(starts), end_tiles=tuple(ends)),
        name="conv_silu", grid=(t // rows, ch // cols),
        in_specs=[pl.BlockSpec((halo, cols), lambda i, j: (jnp.maximum(i * rb - 1, 0), cb + j)),
                  pl.BlockSpec((rows, cols), lambda i, j: (i, cb + j)),
                  pl.BlockSpec((halo, cols), lambda i, j: (jnp.minimum((i + 1) * rb, nhalo - 1), cb + j)),
                  pl.BlockSpec((width, cols), lambda i, j: (0, j)),
                  pl.BlockSpec((1, cols), lambda i, j: (0, j))],
        out_specs=pl.BlockSpec((rows, cols), lambda i, j: (i, j)),
        out_shape=jax.ShapeDtypeStruct((t, ch), BF16),
        compiler_params=_cparams(("parallel", "arbitrary")))(proj, proj, proj, conv_w, conv_b.reshape(1, ch))


def _ssd_kernel(*refs, reverse, n_chunks, reset_chunks, groups, heads, d_inner):
    if reverse:
        (xbc_ref, dt_ref, dtt_ref, dtb_ref, dtbt_ref, alog_ref, alogt_ref, tri_ref, trit_ref, exp_ref,
         yf_ref, z_ref, nw_ref, o_ref, state_ref) = refs
    else:
        (xbc_ref, dt_ref, dtt_ref, dtb_ref, dtbt_ref, alog_ref, alogt_ref, tri_ref, trit_ref, exp_ref,
         dskip_ref, o_ref, state_ref) = refs
    step = pl.program_id(0)
    chunk = (n_chunks - 1 - step) if reverse else step
    L = SSD_CHUNK
    gw = d_inner // groups
    hpg = heads // groups
    n = SSM_STATE
    col0 = heads if reverse else 0

    @pl.when(functools.reduce(jnp.logical_or, [chunk == c for c in reset_chunks]))
    def _():
        state_ref[...] = jnp.zeros_like(state_ref)

    dt = _softplus(dt_ref[...] + dtb_ref[...])
    dta = dt * (-jnp.exp(alog_ref[...]))
    dt_t = _softplus(dtt_ref[...] + dtbt_ref[...])
    dta_t = dt_t * (-jnp.exp(alogt_ref[...]))
    tri = tri_ref[...]
    tri_t = trit_ref[...]
    acum = sum(_dot(tri, p) for p in _split_bf16(dta, 3))
    acum_t = sum(_dot(p, tri_t) for p in _split_bf16(dta_t, 3))
    edge = 0 if reverse else L - 1
    a_end = acum[edge:edge + 1, :]
    to_end = jnp.exp(a_end - acum) * dt
    ea = jnp.exp(acum)
    expand = exp_ref[...]
    to_end_x = sum(_dot(p, expand) for p in _split_bf16(to_end, 2))
    ea_x = sum(_dot(p, expand) for p in _split_bf16(ea, 2))
    decay_x = ea_x[edge:edge + 1, :]

    row = lax.broadcasted_iota(jnp.int32, (L, L), 0)
    colm = lax.broadcasted_iota(jnp.int32, (L, L), 1)
    causal = (row <= colm) if reverse else (row >= colm)
    lane = lax.broadcasted_iota(jnp.int32, (L, LANES), 1)
    first_half = lane < SSM_HEAD_DIM

    for g in range(groups):
        xs = xbc_ref[:, g * gw:(g + 1) * gw]
        bm = xbc_ref[:, d_inner + g * n:d_inner + (g + 1) * n]
        cm = xbc_ref[:, d_inner + groups * n + g * n:d_inner + groups * n + (g + 1) * n]
        cb = _dot_nt(cm, bm)
        st = state_ref[g]
        y_off = _dot(cm, st.astype(BF16)) * ea_x[:, g * gw:(g + 1) * gw]
        xs_scaled = (xs.astype(F32) * to_end_x[:, g * gw:(g + 1) * gw]).astype(BF16)
        state_ref[g] = st * decay_x[:, g * gw:(g + 1) * gw] + _dot_tn(bm, xs_scaled)
        pairs = []
        for k in range(hpg // 2):
            xp = xs[:, k * LANES:(k + 1) * LANES]
            y_pair = None
            for side in range(2):
                c = col0 + g * hpg + 2 * k + side
                seg = acum[:, c:c + 1] - acum_t[c:c + 1, :]
                decay = jnp.exp(jnp.where(causal, seg, -jnp.inf))
                m = (cb * decay * dt_t[c:c + 1, :]).astype(BF16)
                keep = first_half if side == 0 else jnp.logical_not(first_half)
                part = _dot(m, jnp.where(keep, xp, jnp.zeros_like(xp)))
                y_pair = part if y_pair is None else y_pair + part
            lo = g * gw + k * LANES
            y_pair = y_pair + y_off[:, k * LANES:(k + 1) * LANES]
            if not reverse:
                o_ref[:, lo:lo + LANES] = y_pair + dskip_ref[:, lo:lo + LANES] * xp.astype(F32)
            else:
                pairs.append(y_pair)
        if reverse:
            y = jnp.concatenate(pairs, axis=1) + yf_ref[:, g * gw:(g + 1) * gw]
            y = y * _silu(z_ref[:, g * gw:(g + 1) * gw].astype(F32))
            o_ref[:, g * gw:(g + 1) * gw] = (_rms(y, NORM_EPS) * nw_ref[:, g * gw:(g + 1) * gw]).astype(o_ref.dtype)


def _ssd(xbc, proj, dt_raw, dt_raw_t, dt_bias, a_log, d_skip, norm_w, seq_lens, groups):
    t = xbc.shape[0]
    heads = d_skip.shape[0]
    d_inner = heads * SSM_HEAD_DIM
    L = SSD_CHUNK
    nc = t // L
    assert 2 * heads <= LANES and (heads // groups) % 2 == 0
    starts, lasts, pos = [], [], 0
    for s in seq_lens:
        assert s % L == 0
        starts.append(pos // L)
        pos += s
        lasts.append(pos // L - 1)
    padh = LANES - 2 * heads
    dtb = jnp.pad(dt_bias.reshape(-1), (0, padh))
    alog = jnp.pad(a_log.reshape(-1), (0, padh))
    tri_lo = jnp.tril(jnp.ones((L, L), F32)).astype(BF16)
    tri_up = jnp.triu(jnp.ones((L, L), F32)).astype(BF16)
    head_of_col = jnp.arange(d_inner) // SSM_HEAD_DIM
    dskip_x = jnp.repeat(d_skip.astype(F32), SSM_HEAD_DIM).reshape(1, d_inner)

    def run(reverse, extra_args, extra_specs, out_dtype):
        cidx = (lambda i: nc - 1 - i) if reverse else (lambda i: i)
        tri = tri_up if reverse else tri_lo
        expand = (jnp.arange(LANES)[:, None] == (head_of_col + (heads if reverse else 0))[None, :]).astype(BF16)
        const = lambda shape: pl.BlockSpec(shape, lambda i: (0, 0))
        in_specs = [pl.BlockSpec((L, xbc.shape[1]), lambda i: (cidx(i), 0)),
                    pl.BlockSpec((L, LANES), lambda i: (cidx(i), 0)),
                    pl.BlockSpec((LANES, L), lambda i: (0, cidx(i))),
                    const((1, LANES)), const((LANES, 1)), const((1, LANES)), const((LANES, 1)),
                    const((L, L)), const((L, L)), const((LANES, d_inner))] + extra_specs(cidx)
        return pl.pallas_call(
            functools.partial(_ssd_kernel, reverse=reverse, n_chunks=nc,
                              reset_chunks=tuple(lasts if reverse else starts),
                              groups=groups, heads=heads, d_inner=d_inner),
            name="ssd_bwd" if reverse else "ssd_fwd", grid=(nc,), in_specs=in_specs,
            out_specs=pl.BlockSpec((L, d_inner), lambda i: (cidx(i), 0)),
            out_shape=jax.ShapeDtypeStruct((t, d_inner), out_dtype),
            scratch_shapes=[pltpu.VMEM((groups, SSM_STATE, d_inner // groups), F32)],
            compiler_params=_cparams(("arbitrary",)))(
                xbc, dt_raw, dt_raw_t, dtb.reshape(1, LANES), dtb.reshape(LANES, 1), alog.reshape(1, LANES),
                alog.reshape(LANES, 1), tri, tri.T, expand, *extra_args)

    y_fwd = run(False, [dskip_x], lambda cidx: [pl.BlockSpec((1, d_inner), lambda i: (0, 0))], F32)
    return run(True, [y_fwd, proj, norm_w.reshape(1, d_inner)],
               lambda cidx: [pl.BlockSpec((L, d_inner), lambda i: (cidx(i), 0)),
                             pl.BlockSpec((L, d_inner), lambda i: (cidx(i), 0)),
                             pl.BlockSpec((1, d_inner), lambda i: (0, 0))], BF16)


def _t5_bucket(rel):
    half = NUM_BUCKETS // 2
    max_exact = half // 2
    n = jnp.abs(rel)
    log_ratio = jnp.log(jnp.maximum(n, 1).astype(F32) / max_exact) / math.log(MAX_DISTANCE / max_exact)
    large = jnp.minimum(max_exact + (log_ratio * (half - max_exact)).astype(jnp.int32), half - 1)
    return jnp.where(rel > 0, half, 0) + jnp.where(n < max_exact, n, large)


def _attn_kernel(far_ref, lq_ref, sw_ref, q_ref, k_ref, v_ref, bias_ref, o_ref, m_ref, l_ref, acc_ref,
                 sa_ref, sb_ref, *, t, nk, lambda_init):
    h = pl.program_id(1)
    i = pl.program_id(2)
    dh = DIFF_HEAD_DIM
    nblk = t // LANES
    c_before = far_ref[h, 0]
    c_after = far_ref[h, 1]
    m_ref[...] = jnp.full(m_ref.shape, -jnp.inf, F32)
    l_ref[...] = jnp.zeros(l_ref.shape, F32)
    acc_ref[...] = jnp.zeros(acc_ref.shape, F32)

    def qk(kb, j, width=1):
        ks = k_ref[pl.ds(pl.multiple_of(kb * t, t), width * t), j * dh:(j + 1) * dh]
        return _dot_nt(q_ref[:, j * dh:(j + 1) * dh], ks)

    def softmax_pv(kb, s_of, shift, width=1):
        vs = v_ref[pl.ds(pl.multiple_of(kb * t, t), width * t), :]
        for j in range(2):
            s = s_of(j)
            blocks = [s[:, c * LANES:(c + 1) * LANES] for c in range(width * nblk)]
            row_max = jnp.max(functools.reduce(jnp.maximum, blocks), axis=-1, keepdims=True)
            m_old = m_ref[j]
            m_new = jnp.maximum(m_old, row_max + shift)
            alpha = jnp.exp2(m_old - m_new)
            ref = m_new - shift
            ps = [jnp.exp2(b - ref) for b in blocks]
            l_ref[j] = alpha * l_ref[j] + functools.reduce(jnp.add, ps)
            p = jnp.concatenate([x.astype(BF16) for x in ps], axis=1)
            acc_ref[j] = jnp.concatenate([alpha] * (acc_ref.shape[2] // LANES), axis=1) * acc_ref[j] + _dot(p, vs)
            m_ref[j] = m_new

    def update(kb, bias_tile):
        softmax_pv(kb, lambda j: qk(kb, j) + bias_tile, 0.0)

    def far_loop(lo, hi, shift):
        n = hi - lo
        nd = n // 2
        base = lo + n % 2

        @pl.when(n % 2 == 1)
        def _():
            softmax_pv(lo, lambda j: qk(lo, j), shift)

        @pl.when(nd > 0)
        def _():
            for j in range(2):
                sa_ref[j] = qk(base, j, 2)

            def pair(kp, carry):
                d0 = base + 4 * kp
                d1 = jnp.minimum(d0 + 2, hi - 2)
                d2 = jnp.minimum(d0 + 4, hi - 2)
                for j in range(2):
                    sb_ref[j] = qk(d1, j, 2)
                softmax_pv(d0, lambda j: sa_ref[j], shift, 2)
                for j in range(2):
                    sa_ref[j] = qk(d2, j, 2)
                softmax_pv(d1, lambda j: sb_ref[j], shift, 2)
                return carry

            lax.fori_loop(0, nd // 2, pair, 0)

            @pl.when(nd % 2 == 1)
            def _():
                softmax_pv(hi - 2, lambda j: sa_ref[j], shift, 2)

    far_loop(0, jnp.maximum(i - 1, 0), c_before)

    @pl.when(i >= 1)
    def _():
        update(i - 1, bias_ref[0, 0])

    update(i, bias_ref[0, 1])

    @pl.when(i + 1 < nk)
    def _():
        update(i + 1, bias_ref[0, 2])

    far_loop(jnp.minimum(i + 2, nk), nk, c_after)

    lq = lq_ref[...]
    lam = (jnp.exp(jnp.sum(lq[0:1] * lq[1:2], axis=-1, keepdims=True))
           - jnp.exp(jnp.sum(lq[2:3] * lq[3:4], axis=-1, keepdims=True)) + lambda_init)
    l0 = jnp.sum(l_ref[0], axis=-1, keepdims=True)
    l1 = jnp.sum(l_ref[1], axis=-1, keepdims=True)
    o = acc_ref[0] / l0 - lam * (acc_ref[1] / l1)
    o_ref[...] = (_rms(o, SUBLN_EPS) * sw_ref[...] * (1.0 - lambda_init)).astype(o_ref.dtype)


def _near_bias_tiles(table_t, t):
    x = jnp.concatenate([jnp.arange(t), jnp.arange(t) - t])
    rel = (jnp.arange(-1, 2) * t)[:, None] + x[None, :]
    w = table_t[:, _t5_bucket(rel)]
    flat = jnp.tile(w, (1, 1, t))[..., :t * (2 * t - 1)]
    return flat.reshape(w.shape[0], 3, t, 2 * t - 1)[..., :t]


def _diff_attention(proj, q_off, n_heads, rel_bias, lambda_qk, subln_w, lambda_init, tok_off, bsz, seq):
    t = min(ATTN_TILE, seq)
    nk = seq // t
    hw = 2 * DIFF_HEAD_DIM
    assert t >= MAX_DISTANCE and seq % t == 0 and tok_off % seq == 0 and q_off % hw == 0
    qb = q_off // hw
    table_t = rel_bias.astype(F32).T * LOG2E
    bias_near = _near_bias_tiles(table_t, t)
    far = table_t[:, _t5_bucket(jnp.array([-2 * MAX_DISTANCE, 2 * MAX_DISTANCE]))]
    row0 = tok_off // t
    seq0 = tok_off // seq
    return pl.pallas_call(
        functools.partial(_attn_kernel, t=t, nk=nk, lambda_init=lambda_init),
        name="diff_attn", grid=(bsz, n_heads, nk),
        in_specs=[pl.BlockSpec(memory_space=pltpu.SMEM),
                  pl.BlockSpec((4, DIFF_HEAD_DIM), lambda b, h, i: (0, 0)),
                  pl.BlockSpec((1, hw), lambda b, h, i: (0, 0)),
                  pl.BlockSpec((t, hw), lambda b, h, i: (row0 + b * nk + i, qb + h)),
                  pl.BlockSpec((seq, hw), lambda b, h, i: (seq0 + b, qb + n_heads + h)),
                  pl.BlockSpec((seq, hw), lambda b, h, i: (seq0 + b, qb + 2 * n_heads + h)),
                  pl.BlockSpec((1, 3, t, t), lambda b, h, i: (h, 0, 0, 0))],
        out_specs=pl.BlockSpec((t, hw), lambda b, h, i: (b * nk + i, h)),
        out_shape=jax.ShapeDtypeStruct((bsz * seq, n_heads * hw), BF16),
        scratch_shapes=[pltpu.VMEM((2, t, LANES), F32), pltpu.VMEM((2, t, LANES), F32),
                        pltpu.VMEM((2, t, hw), F32), pltpu.VMEM((2, t, 2 * t), F32), pltpu.VMEM((2, t, 2 * t), F32)],
        compiler_params=_cparams(("parallel", "parallel", "arbitrary")))(
            far, lambda_qk.astype(F32), subln_w.reshape(1, hw).astype(F32), proj, proj, proj, bias_near)


def _gelu_tanh(x):
    return 0.5 * x * (1.0 + jnp.tanh(math.sqrt(2.0 / math.pi) * (x + 0.044715 * (x * x * x))))


def _gmlp_kernel(u_ref, v_ref, lng_ref, lnb_ref, ws_ref, bs_ref, o_ref, *, rows, groups):
    u = _gelu_tanh(u_ref[...].astype(F32))
    v = _gelu_tanh(v_ref[...].astype(F32))
    mu = jnp.mean(v, axis=-1, keepdims=True)
    vc = v - mu
    vn = (vc * lax.rsqrt(jnp.mean(vc * vc, axis=-1, keepdims=True) + LN_EPS) * lng_ref[...] + lnb_ref[...]).astype(BF16)
    gd = v.shape[1] // groups
    L = GMLP_CHUNK
    for c in range(rows // L):
        for g in range(groups):
            mixed = _dot(ws_ref[g], vn[c * L:(c + 1) * L, g * gd:(g + 1) * gd]) + bs_ref[:, g * gd:(g + 1) * gd]
            o_ref[c * L:(c + 1) * L, g * gd:(g + 1) * gd] = (
                u[c * L:(c + 1) * L, g * gd:(g + 1) * gd] * mixed).astype(o_ref.dtype)


def _gmlp(proj, u_off, width, ln_v, w_s, b_s):
    t = proj.shape[0]
    groups = w_s.shape[0]
    L = GMLP_CHUNK
    rows = min(GMLP_ROWS, t)
    assert u_off % width == 0 and rows % L == 0
    ub = u_off // width
    bias_x = jnp.repeat(b_s.astype(F32).T, width // groups, axis=1)
    vec = pl.BlockSpec((1, width), lambda i: (0, 0))
    return pl.pallas_call(
        functools.partial(_gmlp_kernel, rows=rows, groups=groups), name="gmlp", grid=(t // rows,),
        in_specs=[pl.BlockSpec((rows, width), lambda i: (i, ub)),
                  pl.BlockSpec((rows, width), lambda i: (i, ub + 1)),
                  vec, vec,
                  pl.BlockSpec((groups, L, L), lambda i: (0, 0, 0)),
                  pl.BlockSpec((L, width), lambda i: (0, 0))],
        out_specs=pl.BlockSpec((rows, width), lambda i: (i, 0)),
        out_shape=jax.ShapeDtypeStruct((t, width), BF16),
        compiler_params=_cparams(("parallel",)))(
            proj, proj, ln_v[0].reshape(1, width).astype(F32), ln_v[1].reshape(1, width).astype(F32),
            w_s.astype(BF16), bias_x)


def _merge_kernel(h_ref, s_ref, a_ref, m_ref, wg0_ref, wg1_ref, wg2_ref, wos_ref, woa_ref, wom_ref, o_ref):
    h = h_ref[...]
    merged = (jax.nn.sigmoid(_dot(h, wg0_ref[...])) * _dot(s_ref[...], wos_ref[...])
              + jax.nn.sigmoid(_dot(h, wg1_ref[...])) * _dot(a_ref[...], woa_ref[...])
              + jax.nn.sigmoid(_dot(h, wg2_ref[...])) * _dot(m_ref[...], wom_ref[...]))
    o_ref[...] = merged.astype(o_ref.dtype)


def _merge(h, y_s, y_a, y_m, w_gate, w_os, w_oa, w_om):
    t, d = h.shape
    tm = min(MERGE_TM, t)
    tn = min(MERGE_TN, d)
    nj = d // tn
    act = lambda a: pl.BlockSpec((tm, a.shape[1]), lambda i, j: (i, 0))
    wcol = lambda w, off: pl.BlockSpec((w.shape[0], tn), lambda i, j: (0, j + off))
    return pl.pallas_call(
        _merge_kernel, name="merge", grid=(t // tm, nj),
        in_specs=[act(h), act(y_s), act(y_a), act(y_m),
                  wcol(w_gate, 0), wcol(w_gate, nj), wcol(w_gate, 2 * nj),
                  wcol(w_os, 0), wcol(w_oa, 0), wcol(w_om, 0)],
        out_specs=pl.BlockSpec((tm, tn), lambda i, j: (i, j)),
        out_shape=jax.ShapeDtypeStruct((t, d), BF16),
        compiler_params=_cparams(("parallel", "arbitrary")))(
            h, y_s, y_a, y_m, w_gate, w_gate, w_gate, w_os, w_oa, w_om)


def kernel(x_prompt, x_sample, rel_bias, norms, w_ffn1_in, w_ffn1_out, w_in, conv_w, conv_b, dt_bias, a_log,
           d_skip, ssm_norm, w_o_ssm, lambda_qk, diff_subln, w_o_diff, ln_v, w_spatial, b_spatial, w_o_gmlp,
           w_out, w_ffn2_in, w_ffn2_out):
    depth = norms.shape[0]
    d = x_prompt.shape[-1]
    d_inner = w_o_ssm.shape[1]
    conv_ch = conv_w.shape[2]
    heads = d_skip.shape[1]
    groups = (conv_ch - d_inner) // (2 * SSM_STATE)
    diff_w = w_o_diff.shape[1]
    diff_heads = diff_w // (2 * DIFF_HEAD_DIM)
    gmlp_w = w_o_gmlp.shape[1]
    pb, ps = x_prompt.shape[:2]
    sb, ss = x_sample.shape[:2]
    seq_lens = [ps] * pb + [ss] * sb
    n_prompt = pb * ps

    c_dt = d_inner + conv_ch
    c_diff = c_dt + 2 * heads
    c_gmlp = c_diff + 3 * diff_w
    c_gate = c_gmlp + 2 * gmlp_w
    q_off = c_dt
    u_off = q_off + 3 * diff_w
    n_main = u_off + 2 * gmlp_w
    col_scale = jnp.ones((1, n_main), F32).at[:, q_off:q_off + diff_w].set(DIFF_HEAD_DIM ** -0.5 * LOG2E)

    x = [x_prompt.reshape(n_prompt, d), x_sample.reshape(sb * ss, d)]
    h = _rmsnorm(x, norms[0, 0])
    for l in range(depth):
        n = norms[l]
        lambda_init = 0.8 - 0.6 * math.exp(-0.3 * l)
        wi = w_in[l]
        w_main = jnp.concatenate([wi[:, :c_dt], wi[:, c_diff:c_gate]], axis=1).astype(BF16)
        w_dt = jnp.pad(wi[:, c_dt:c_diff], ((0, 0), (0, LANES - 2 * heads))).astype(BF16)
        w_gate = wi[:, c_gate:].astype(BF16)

        y = _matmul(_matmul_swiglu(h, w_ffn1_in[l].astype(BF16)), w_ffn1_out[l].astype(BF16), F32)
        x, h = _resid_norm(x, y, n[1], 0.5, n[2])
        x = [x]

        proj = _matmul(h, w_main, BF16, scale=col_scale)
        dt_raw = _matmul(h, w_dt, F32, tn=LANES)
        dt_raw_t = _matmul_t(w_dt.T, h)
        xbc = _conv_silu(proj, d_inner, conv_w[l], conv_b[l], seq_lens)
        y_ssm = _ssd(xbc, proj, dt_raw, dt_raw_t, dt_bias[l], a_log[l], d_skip[l], ssm_norm[l], seq_lens, groups)
        y_att = jnp.concatenate([
            _diff_attention(proj, q_off, diff_heads, rel_bias, lambda_qk[l], diff_subln[l], lambda_init, 0, pb, ps),
            _diff_attention(proj, q_off, diff_heads, rel_bias, lambda_qk[l], diff_subln[l], lambda_init,
                            n_prompt, sb, ss)], axis=0)
        y_gmlp = _gmlp(proj, u_off, gmlp_w, ln_v[l], w_spatial[l], b_spatial[l])
        merged = _merge(h, y_ssm, y_att, y_gmlp, w_gate, w_o_ssm[l].astype(BF16), w_o_diff[l].astype(BF16),
                        w_o_gmlp[l].astype(BF16))
        y = _matmul(merged, w_out[l].astype(BF16), F32)
        x, h = _resid_norm(x, y, n[3], 1.0, n[4])
        x = [x]

        y = _matmul(_matmul_swiglu(h, w_ffn2_in[l].astype(BF16)), w_ffn2_out[l].astype(BF16), F32)
        if l + 1 < depth:
            x, h = _resid_norm(x, y, n[5], 0.5, norms[l + 1, 0])
            x = [x]

    y_prompt = _resid_rows(x[0], y, norms[depth - 1, 5], 0.5, 0, n_prompt)
    y_sample = _resid_rows(x[0], y, norms[depth - 1, 5], 0.5, n_prompt, sb * ss)
    return y_prompt.reshape(x_prompt.shape), y_sample.reshape(x_sample.shape)
```

```python
import functools
import math

import jax
import jax.numpy as jnp
from jax import lax
from jax.experimental import pallas as pl
from jax.experimental.pallas import tpu as pltpu

F32 = jnp.float32
BF16 = jnp.bfloat16

SSM_HEAD_DIM = 64
SSM_STATE = 128
SSD_CHUNK = 128
DIFF_HEAD_DIM = 128
NUM_BUCKETS = 32
MAX_DISTANCE = 128
GMLP_CHUNK = 128
N_BRANCHES = 3
NORM_EPS = 1e-6
SUBLN_EPS = 1e-5
LN_EPS = 1e-5
LOG2E = math.log2(math.e)

LANES = 128
BF16_SUBLANES = 16
VMEM_LIMIT_BYTES = 56 * 1024 * 1024

MM_TM = 1024
MM_TN = 1024
ROW_TILE = 256
CONV_ROWS = 512
CONV_COLS = 512
CONV_BLOCK = 128
ATTN_TILE = 512
FAR_W = 4
MERGE_TM = 512
MERGE_TN = 256
GMLP_ROWS = 256


def _cparams(sem):
    return pltpu.CompilerParams(dimension_semantics=sem, vmem_limit_bytes=VMEM_LIMIT_BYTES)


def _dot(a, b):
    return jnp.dot(a, b, preferred_element_type=F32)


def _dot_nt(a, b):
    return lax.dot_general(a, b, (((1,), (1,)), ((), ())), preferred_element_type=F32)


def _dot_tn(a, b):
    return lax.dot_general(a, b, (((0,), (0,)), ((), ())), preferred_element_type=F32)


def _split_bf16(x, n):
    parts = []
    r = x
    for _ in range(n):
        p = r.astype(BF16)
        parts.append(p)
        r = r - p.astype(F32)
    return parts


def _rms(x, eps):
    return x * lax.rsqrt(jnp.mean(x * x, axis=-1, keepdims=True) + eps)


def _silu(x):
    return x * jax.nn.sigmoid(x)


def _softplus(x):
    return jnp.maximum(x, 0.0) + jnp.log1p(jnp.exp(-jnp.abs(x)))


def _mm_scale_kernel(a_ref, w_ref, s_ref, o_ref):
    o_ref[...] = (_dot(a_ref[...], w_ref[...]) * s_ref[...]).astype(o_ref.dtype)


def _mm_kernel(a_ref, w_ref, o_ref):
    o_ref[...] = _dot(a_ref[...], w_ref[...]).astype(o_ref.dtype)


def _mm_swiglu_kernel(a_ref, wg_ref, wu_ref, o_ref):
    a = a_ref[...]
    g = _dot(a, wg_ref[...])
    u = _dot(a, wu_ref[...])
    o_ref[...] = (_silu(g) * u).astype(o_ref.dtype)


def _matmul(a, w, out_dtype, scale=None, tm=None, tn=None):
    m, k = a.shape
    n = w.shape[1]
    tm = min(tm or MM_TM, m)
    tn = min(tn or MM_TN, n)
    in_specs = [pl.BlockSpec((tm, k), lambda i, j: (i, 0)), pl.BlockSpec((k, tn), lambda i, j: (0, j))]
    args = [a, w]
    body = _mm_kernel
    if scale is not None:
        in_specs.append(pl.BlockSpec((1, tn), lambda i, j: (0, j)))
        args.append(scale)
        body = _mm_scale_kernel
    return pl.pallas_call(
        body, name="mm", grid=(m // tm, n // tn), in_specs=in_specs,
        out_specs=pl.BlockSpec((tm, tn), lambda i, j: (i, j)),
        out_shape=jax.ShapeDtypeStruct((m, n), out_dtype),
        compiler_params=_cparams(("parallel", "arbitrary")))(*args)


def _matmul_swiglu(a, w):
    m, k = a.shape
    n = w.shape[1] // 2
    tm = min(MM_TM, m)
    tn = min(MM_TN // 2, n)
    nj = n // tn
    return pl.pallas_call(
        _mm_swiglu_kernel, name="mm_swiglu", grid=(m // tm, nj),
        in_specs=[pl.BlockSpec((tm, k), lambda i, j: (i, 0)),
                  pl.BlockSpec((k, tn), lambda i, j: (0, j)),
                  pl.BlockSpec((k, tn), lambda i, j: (0, j + nj))],
        out_specs=pl.BlockSpec((tm, tn), lambda i, j: (i, j)),
        out_shape=jax.ShapeDtypeStruct((m, n), BF16),
        compiler_params=_cparams(("parallel", "arbitrary")))(a, w, w)


def _stream_specs(parts, tr):
    d = parts[0].shape[1]
    if len(parts) == 1:
        return [pl.BlockSpec((tr, d), lambda i: (i, 0))], None
    na = parts[0].shape[0] // tr
    return [pl.BlockSpec((tr, d), lambda i: (jnp.minimum(i, na - 1), 0)),
            pl.BlockSpec((tr, d), lambda i: (jnp.maximum(i - na, 0), 0))], na


def _stream_block(x_refs, first_blocks):
    if len(x_refs) == 1:
        return x_refs[0][...]
    return jnp.where(pl.program_id(0) < first_blocks, x_refs[0][...], x_refs[1][...])


def _rmsnorm_kernel(*refs, n_parts, first_blocks):
    g_ref, o_ref = refs[n_parts:]
    o_ref[...] = (_rms(_stream_block(refs[:n_parts], first_blocks), NORM_EPS) * g_ref[...]).astype(o_ref.dtype)


def _rmsnorm(parts, g):
    d = parts[0].shape[1]
    t = sum(p.shape[0] for p in parts)
    tr = min(ROW_TILE, min(p.shape[0] for p in parts))
    x_specs, first_blocks = _stream_specs(parts, tr)
    return pl.pallas_call(
        functools.partial(_rmsnorm_kernel, n_parts=len(parts), first_blocks=first_blocks),
        name="rmsnorm", grid=(t // tr,),
        in_specs=x_specs + [pl.BlockSpec((1, d), lambda i: (0, 0))],
        out_specs=pl.BlockSpec((tr, d), lambda i: (i, 0)),
        out_shape=jax.ShapeDtypeStruct((t, d), BF16),
        compiler_params=_cparams(("parallel",)))(*parts, g.reshape(1, d))


def _resid_norm_kernel(*refs, n_parts, first_blocks, scale):
    y_ref, g1_ref, g2_ref, xo_ref, h_ref = refs[n_parts:]
    xn = _stream_block(refs[:n_parts], first_blocks) + scale * (_rms(y_ref[...].astype(F32), NORM_EPS) * g1_ref[...])
    xo_ref[...] = xn
    h_ref[...] = (_rms(xn, NORM_EPS) * g2_ref[...]).astype(h_ref.dtype)


def _resid_kernel(x_ref, y_ref, g1_ref, xo_ref, *, scale):
    xo_ref[...] = x_ref[...] + scale * (_rms(y_ref[...].astype(F32), NORM_EPS) * g1_ref[...])


def _resid_norm(parts, y, g_post, scale, g_next):
    t, d = y.shape
    tr = min(ROW_TILE, min(p.shape[0] for p in parts))
    x_specs, first_blocks = _stream_specs(parts, tr)
    row = pl.BlockSpec((tr, d), lambda i: (i, 0))
    vec = pl.BlockSpec((1, d), lambda i: (0, 0))
    return pl.pallas_call(
        functools.partial(_resid_norm_kernel, n_parts=len(parts), first_blocks=first_blocks, scale=scale),
        name="resid_norm", grid=(t // tr,),
        in_specs=x_specs + [row, vec, vec], out_specs=[row, row],
        out_shape=[jax.ShapeDtypeStruct((t, d), F32), jax.ShapeDtypeStruct((t, d), BF16)],
        compiler_params=_cparams(("parallel",)))(*parts, y, g_post.reshape(1, d), g_next.reshape(1, d))


def _resid_rows(x, y, g_post, scale, row_off, n_rows):
    d = x.shape[1]
    tr = min(ROW_TILE, n_rows)
    assert row_off % tr == 0 and n_rows % tr == 0
    b0 = row_off // tr
    row = pl.BlockSpec((tr, d), lambda i: (b0 + i, 0))
    return pl.pallas_call(
        functools.partial(_resid_kernel, scale=scale), name="resid", grid=(n_rows // tr,),
        in_specs=[row, row, pl.BlockSpec((1, d), lambda i: (0, 0))],
        out_specs=pl.BlockSpec((tr, d), lambda i: (i, 0)),
        out_shape=jax.ShapeDtypeStruct((n_rows, d), F32),
        compiler_params=_cparams(("parallel",)))(x, y, g_post.reshape(1, d))


def _conv_kernel(prev_ref, x_ref, next_ref, w_ref, b_ref, sh_ref, o_ref, *, rows, width, start_tiles, end_tiles):
    i = pl.program_id(0)
    halo = BF16_SUBLANES
    pad = width // 2
    blk = CONV_BLOCK
    is_start = functools.reduce(jnp.logical_or, [i == s for s in start_tiles])
    is_end = functools.reduce(jnp.logical_or, [i == e for e in end_tiles])
    prev = prev_ref[...]
    nxt = next_ref[...]
    ext = jnp.concatenate([jnp.where(is_start, jnp.zeros_like(prev), prev), x_ref[...],
                           jnp.where(is_end, jnp.zeros_like(nxt), nxt)], axis=0)
    w = w_ref[...]
    bias = b_ref[...]
    select = sh_ref[...]
    taps = [k for k in range(width) if k != pad]
    for rb in range(rows // blk):
        shifted = _dot(select, ext[rb * blk:rb * blk + blk + 2 * halo, :])
        acc = bias + w[pad:pad + 1, :] * x_ref[rb * blk:(rb + 1) * blk, :].astype(F32)
        for n, k in enumerate(taps):
            acc = acc + w[k:k + 1, :] * shifted[n * blk:(n + 1) * blk, :]
        o_ref[rb * blk:(rb + 1) * blk, :] = _silu(acc).astype(o_ref.dtype)


def _conv_silu(proj, col_off, conv_w, conv_b, seq_lens):
    t = proj.shape[0]
    width, ch = conv_w.shape
    rows = min(CONV_ROWS, min(seq_lens))
    cols = min(CONV_COLS, ch)
    halo = BF16_SUBLANES
    blk = CONV_BLOCK
    pad = width // 2
    assert col_off % cols == 0 and ch % cols == 0 and all(s % rows == 0 for s in seq_lens)
    assert rows % blk == 0 and pad <= halo
    starts, ends, pos = [], [], 0
    for s in seq_lens:
        starts.append(pos // rows)
        pos += s
        ends.append(pos // rows - 1)
    rb = rows // halo
    nhalo = t // halo
    cb = col_off // cols
    taps = jnp.array([k for k in range(width) if k != pad])
    src = jnp.arange(blk)[None, :] + halo + (taps - pad)[:, None]
    select = (src.reshape(-1)[:, None] == jnp.arange(blk + 2 * halo)[None, :]).astype(BF16)
    return pl.pallas_call(
        functools.partial(_conv_kernel, rows=rows, width=width, start_tiles=tuple(starts), end_tiles=tuple(ends)),
        name="conv_silu", grid=(t // rows, ch // cols),
        in_specs=[pl.BlockSpec((halo, cols), lambda i, j: (jnp.maximum(i * rb - 1, 0), cb + j)),
                  pl.BlockSpec((rows, cols), lambda i, j: (i, cb + j)),
                  pl.BlockSpec((halo, cols), lambda i, j: (jnp.minimum((i + 1) * rb, nhalo - 1), cb + j)),
                  pl.BlockSpec((width, cols), lambda i, j: (0, j)),
                  pl.BlockSpec((1, cols), lambda i, j: (0, j)),
                  pl.BlockSpec(select.shape, lambda i, j: (0, 0))],
        out_specs=pl.BlockSpec((rows, cols), lambda i, j: (i, j)),
        out_shape=jax.ShapeDtypeStruct((t, ch), BF16),
        compiler_params=_cparams(("parallel", "arbitrary")))(
            proj, proj, proj, conv_w, conv_b.reshape(1, ch), select)


def _ssd_kernel(*refs, reverse, n_chunks, reset_chunks, groups, heads, d_inner):
    if reverse:
        (xbc_ref, dt_ref, dtb_ref, alog_ref, tri_ref, trit_ref, exp_ref,
         yf_ref, z_ref, nw_ref, o_ref, state_ref) = refs
    else:
        (xbc_ref, dt_ref, dtb_ref, alog_ref, tri_ref, trit_ref, exp_ref,
         dskip_ref, o_ref, state_ref) = refs
    step = pl.program_id(0)
    chunk = (n_chunks - 1 - step) if reverse else step
    L = SSD_CHUNK
    gw = d_inner // groups
    hpg = heads // groups
    n = SSM_STATE
    col0 = heads if reverse else 0

    @pl.when(functools.reduce(jnp.logical_or, [chunk == c for c in reset_chunks]))
    def _():
        state_ref[...] = jnp.zeros_like(state_ref)

    dt = _softplus(dt_ref[...] + dtb_ref[...])
    dta = dt * (-jnp.exp(alog_ref[...]))
    dt_t = dt.T
    dta_t = dta.T
    tri = tri_ref[...]
    tri_t = trit_ref[...]
    acum = sum(_dot(tri, p) for p in _split_bf16(dta, 3))
    acum_t = sum(_dot(p, tri_t) for p in _split_bf16(dta_t, 3))
    edge = 0 if reverse else L - 1
    a_end = acum[edge:edge + 1, :]
    to_end = jnp.exp(a_end - acum) * dt
    ea = jnp.exp(acum)
    expand = exp_ref[...]
    to_end_x = sum(_dot(p, expand) for p in _split_bf16(to_end, 2))
    ea_x = sum(_dot(p, expand) for p in _split_bf16(ea, 2))
    decay_x = ea_x[edge:edge + 1, :]

    row = lax.broadcasted_iota(jnp.int32, (L, L), 0)
    colm = lax.broadcasted_iota(jnp.int32, (L, L), 1)
    causal = (row <= colm) if reverse else (row >= colm)
    lane = lax.broadcasted_iota(jnp.int32, (L, LANES), 1)
    first_half = lane < SSM_HEAD_DIM

    for g in range(groups):
        xs = xbc_ref[:, g * gw:(g + 1) * gw]
        bm = xbc_ref[:, d_inner + g * n:d_inner + (g + 1) * n]
        cm = xbc_ref[:, d_inner + groups * n + g * n:d_inner + groups * n + (g + 1) * n]
        cb = _dot_nt(cm, bm)
        st = state_ref[g]
        y_off = _dot(cm, st.astype(BF16)) * ea_x[:, g * gw:(g + 1) * gw]
        xs_scaled = (xs.astype(F32) * to_end_x[:, g * gw:(g + 1) * gw]).astype(BF16)
        state_ref[g] = st * decay_x[:, g * gw:(g + 1) * gw] + _dot_tn(bm, xs_scaled)
        pairs = []
        for k in range(hpg // 2):
            xp = xs[:, k * LANES:(k + 1) * LANES]
            y_pair = None
            for side in range(2):
                c = col0 + g * hpg + 2 * k + side
                seg = acum[:, c:c + 1] - acum_t[c:c + 1, :]
                decay = jnp.exp(jnp.where(causal, seg, -jnp.inf))
                m = (cb * decay * dt_t[c:c + 1, :]).astype(BF16)
                keep = first_half if side == 0 else jnp.logical_not(first_half)
                part = _dot(m, jnp.where(keep, xp, jnp.zeros_like(xp)))
                y_pair = part if y_pair is None else y_pair + part
            lo = g * gw + k * LANES
            y_pair = y_pair + y_off[:, k * LANES:(k + 1) * LANES]
            if not reverse:
                o_ref[:, lo:lo + LANES] = y_pair + dskip_ref[:, lo:lo + LANES] * xp.astype(F32)
            else:
                pairs.append(y_pair)
        if reverse:
            y = jnp.concatenate(pairs, axis=1) + yf_ref[:, g * gw:(g + 1) * gw]
            y = y * _silu(z_ref[:, g * gw:(g + 1) * gw].astype(F32))
            o_ref[:, g * gw:(g + 1) * gw] = (_rms(y, NORM_EPS) * nw_ref[:, g * gw:(g + 1) * gw]).astype(o_ref.dtype)


def _ssd(xbc, proj, dt_raw, dt_bias, a_log, d_skip, norm_w, seq_lens, groups):
    t = xbc.shape[0]
    heads = d_skip.shape[0]
    d_inner = heads * SSM_HEAD_DIM
    L = SSD_CHUNK
    nc = t // L
    assert 2 * heads <= LANES and (heads // groups) % 2 == 0
    starts, lasts, pos = [], [], 0
    for s in seq_lens:
        assert s % L == 0
        starts.append(pos // L)
        pos += s
        lasts.append(pos // L - 1)
    padh = LANES - 2 * heads
    dtb = jnp.pad(dt_bias.reshape(-1), (0, padh))
    alog = jnp.pad(a_log.reshape(-1), (0, padh))
    tri_lo = jnp.tril(jnp.ones((L, L), F32)).astype(BF16)
    tri_up = jnp.triu(jnp.ones((L, L), F32)).astype(BF16)
    head_of_col = jnp.arange(d_inner) // SSM_HEAD_DIM
    dskip_x = jnp.repeat(d_skip.astype(F32), SSM_HEAD_DIM).reshape(1, d_inner)

    def run(reverse, extra_args, extra_specs, out_dtype):
        cidx = (lambda i: nc - 1 - i) if reverse else (lambda i: i)
        tri = tri_up if reverse else tri_lo
        expand = (jnp.arange(LANES)[:, None] == (head_of_col + (heads if reverse else 0))[None, :]).astype(BF16)
        const = lambda shape: pl.BlockSpec(shape, lambda i: (0, 0))
        in_specs = [pl.BlockSpec((L, xbc.shape[1]), lambda i: (cidx(i), 0)),
                    pl.BlockSpec((L, LANES), lambda i: (cidx(i), 0)),
                    const((1, LANES)), const((1, LANES)),
                    const((L, L)), const((L, L)), const((LANES, d_inner))] + extra_specs(cidx)
        return pl.pallas_call(
            functools.partial(_ssd_kernel, reverse=reverse, n_chunks=nc,
                              reset_chunks=tuple(lasts if reverse else starts),
                              groups=groups, heads=heads, d_inner=d_inner),
            name="ssd_bwd" if reverse else "ssd_fwd", grid=(nc,), in_specs=in_specs,
            out_specs=pl.BlockSpec((L, d_inner), lambda i: (cidx(i), 0)),
            out_shape=jax.ShapeDtypeStruct((t, d_inner), out_dtype),
            scratch_shapes=[pltpu.VMEM((groups, SSM_STATE, d_inner // groups), F32)],
            compiler_params=_cparams(("arbitrary",)))(
                xbc, dt_raw, dtb.reshape(1, LANES), alog.reshape(1, LANES), tri, tri.T, expand, *extra_args)

    y_fwd = run(False, [dskip_x], lambda cidx: [pl.BlockSpec((1, d_inner), lambda i: (0, 0))], F32)
    return run(True, [y_fwd, proj, norm_w.reshape(1, d_inner)],
               lambda cidx: [pl.BlockSpec((L, d_inner), lambda i: (cidx(i), 0)),
                             pl.BlockSpec((L, d_inner), lambda i: (cidx(i), 0)),
                             pl.BlockSpec((1, d_inner), lambda i: (0, 0))], BF16)


def _t5_bucket(rel):
    half = NUM_BUCKETS // 2
    max_exact = half // 2
    n = jnp.abs(rel)
    log_ratio = jnp.log(jnp.maximum(n, 1).astype(F32) / max_exact) / math.log(MAX_DISTANCE / max_exact)
    large = jnp.minimum(max_exact + (log_ratio * (half - max_exact)).astype(jnp.int32), half - 1)
    return jnp.where(rel > 0, half, 0) + jnp.where(n < max_exact, n, large)


def _attn_kernel(far_ref, lq_ref, sw_ref, q_ref, k_ref, v_ref, bias_ref, o_ref, m_ref, l_ref, acc_ref,
                 sa_ref, sb_ref, *, t, nk, lambda_init):
    h = pl.program_id(1)
    i = pl.program_id(2)
    dh = DIFF_HEAD_DIM
    nblk = t // LANES
    c_before = far_ref[h, 0]
    c_after = far_ref[h, 1]
    m_ref[...] = jnp.full(m_ref.shape, -jnp.inf, F32)
    l_ref[...] = jnp.zeros(l_ref.shape, F32)
    acc_ref[...] = jnp.zeros(acc_ref.shape, F32)

    def qk(kb, j, width=1):
        ks = k_ref[pl.ds(pl.multiple_of(kb * t, t), width * t), j * dh:(j + 1) * dh]
        return _dot_nt(q_ref[:, j * dh:(j + 1) * dh], ks)

    def softmax_pv(kb, s_of, shift, width=1):
        vs = v_ref[pl.ds(pl.multiple_of(kb * t, t), width * t), :]
        for j in range(2):
            s = s_of(j)
            blocks = [s[:, c * LANES:(c + 1) * LANES] for c in range(width * nblk)]
            row_max = jnp.max(functools.reduce(jnp.maximum, blocks), axis=-1, keepdims=True)
            m_old = m_ref[j]
            m_new = jnp.maximum(m_old, row_max + shift)
            alpha = jnp.exp2(m_old - m_new)
            ref = m_new - shift
            ps = [jnp.exp2(b - ref) for b in blocks]
            l_ref[j] = alpha * l_ref[j] + functools.reduce(jnp.add, ps)
            p = jnp.concatenate([x.astype(BF16) for x in ps], axis=1)
            acc_ref[j] = jnp.concatenate([alpha] * (acc_ref.shape[2] // LANES), axis=1) * acc_ref[j] + _dot(p, vs)
            m_ref[j] = m_new

    def update(kb, bias_tile):
        softmax_pv(kb, lambda j: qk(kb, j) + bias_tile, 0.0)

    def far_loop(lo, hi, shift):
        w = FAR_W
        n = hi - lo
        nd = n // w
        base = lo + n % w

        def single(kb, carry):
            softmax_pv(kb, lambda j: qk(kb, j), shift)
            return carry

        lax.fori_loop(lo, base, single, 0)

        @pl.when(nd > 0)
        def _():
            for j in range(2):
                sa_ref[j] = qk(base, j, w)

            def pair(kp, carry):
                d0 = base + 2 * w * kp
                d1 = jnp.minimum(d0 + w, hi - w)
                d2 = jnp.minimum(d0 + 2 * w, hi - w)
                for j in range(2):
                    sb_ref[j] = qk(d1, j, w)
                softmax_pv(d0, lambda j: sa_ref[j], shift, w)
                for j in range(2):
                    sa_ref[j] = qk(d2, j, w)
                softmax_pv(d1, lambda j: sb_ref[j], shift, w)
                return carry

            lax.fori_loop(0, nd // 2, pair, 0)

            @pl.when(nd % 2 == 1)
            def _():
                softmax_pv(hi - w, lambda j: sa_ref[j], shift, w)

    far_loop(0, jnp.maximum(i - 1, 0), c_before)

    @pl.when(i >= 1)
    def _():
        update(i - 1, bias_ref[0, 0])

    update(i, bias_ref[0, 1])

    @pl.when(i + 1 < nk)
    def _():
        update(i + 1, bias_ref[0, 2])

    far_loop(jnp.minimum(i + 2, nk), nk, c_after)

    lq = lq_ref[...]
    lam = (jnp.exp(jnp.sum(lq[0:1] * lq[1:2], axis=-1, keepdims=True))
           - jnp.exp(jnp.sum(lq[2:3] * lq[3:4], axis=-1, keepdims=True)) + lambda_init)
    l0 = jnp.sum(l_ref[0], axis=-1, keepdims=True)
    l1 = jnp.sum(l_ref[1], axis=-1, keepdims=True)
    o = acc_ref[0] / l0 - lam * (acc_ref[1] / l1)
    o_ref[...] = (_rms(o, SUBLN_EPS) * sw_ref[...] * (1.0 - lambda_init)).astype(o_ref.dtype)


def _near_bias_tiles(table_t, t):
    x = jnp.concatenate([jnp.arange(t), jnp.arange(t) - t])
    rel = (jnp.arange(-1, 2) * t)[:, None] + x[None, :]
    w = table_t[:, _t5_bucket(rel)]
    flat = jnp.tile(w, (1, 1, t))[..., :t * (2 * t - 1)]
    return flat.reshape(w.shape[0], 3, t, 2 * t - 1)[..., :t]


def _diff_attention(proj, q_off, n_heads, rel_bias, lambda_qk, subln_w, lambda_init, tok_off, bsz, seq):
    t = min(ATTN_TILE, seq)
    nk = seq // t
    hw = 2 * DIFF_HEAD_DIM
    assert t >= MAX_DISTANCE and seq % t == 0 and tok_off % seq == 0 and q_off % hw == 0
    qb = q_off // hw
    table_t = rel_bias.astype(F32).T * LOG2E
    bias_near = _near_bias_tiles(table_t, t)
    far = table_t[:, _t5_bucket(jnp.array([-2 * MAX_DISTANCE, 2 * MAX_DISTANCE]))]
    row0 = tok_off // t
    seq0 = tok_off // seq
    return pl.pallas_call(
        functools.partial(_attn_kernel, t=t, nk=nk, lambda_init=lambda_init),
        name="diff_attn", grid=(bsz, n_heads, nk),
        in_specs=[pl.BlockSpec(memory_space=pltpu.SMEM),
                  pl.BlockSpec((4, DIFF_HEAD_DIM), lambda b, h, i: (0, 0)),
                  pl.BlockSpec((1, hw), lambda b, h, i: (0, 0)),
                  pl.BlockSpec((t, hw), lambda b, h, i: (row0 + b * nk + i, qb + h)),
                  pl.BlockSpec((seq, hw), lambda b, h, i: (seq0 + b, qb + n_heads + h),
                               pipeline_mode=pl.Buffered(1)),
                  pl.BlockSpec((seq, hw), lambda b, h, i: (seq0 + b, qb + 2 * n_heads + h),
                               pipeline_mode=pl.Buffered(1)),
                  pl.BlockSpec((1, 3, t, t), lambda b, h, i: (h, 0, 0, 0))],
        out_specs=pl.BlockSpec((t, hw), lambda b, h, i: (b * nk + i, h)),
        out_shape=jax.ShapeDtypeStruct((bsz * seq, n_heads * hw), BF16),
        scratch_shapes=[pltpu.VMEM((2, t, LANES), F32), pltpu.VMEM((2, t, LANES), F32),
                        pltpu.VMEM((2, t, hw), F32), pltpu.VMEM((2, t, FAR_W * t), F32),
                        pltpu.VMEM((2, t, FAR_W * t), F32)],
        compiler_params=_cparams(("parallel", "parallel", "arbitrary")))(
            far, lambda_qk.astype(F32), subln_w.reshape(1, hw).astype(F32), proj, proj, proj, bias_near)


def _gelu_tanh(x):
    return 0.5 * x * (1.0 + jnp.tanh(math.sqrt(2.0 / math.pi) * (x + 0.044715 * (x * x * x))))


def _gmlp_kernel(u_ref, v_ref, lng_ref, lnb_ref, ws_ref, bs_ref, o_ref, *, rows, groups):
    u = _gelu_tanh(u_ref[...].astype(F32))
    v = _gelu_tanh(v_ref[...].astype(F32))
    mu = jnp.mean(v, axis=-1, keepdims=True)
    vc = v - mu
    vn = (vc * lax.rsqrt(jnp.mean(vc * vc, axis=-1, keepdims=True) + LN_EPS) * lng_ref[...] + lnb_ref[...]).astype(BF16)
    gd = v.shape[1] // groups
    L = GMLP_CHUNK
    for c in range(rows // L):
        for g in range(groups):
            mixed = _dot(ws_ref[g], vn[c * L:(c + 1) * L, g * gd:(g + 1) * gd]) + bs_ref[:, g * gd:(g + 1) * gd]
            o_ref[c * L:(c + 1) * L, g * gd:(g + 1) * gd] = (
                u[c * L:(c + 1) * L, g * gd:(g + 1) * gd] * mixed).astype(o_ref.dtype)


def _gmlp(proj, u_off, width, ln_v, w_s, b_s):
    t = proj.shape[0]
    groups = w_s.shape[0]
    L = GMLP_CHUNK
    rows = min(GMLP_ROWS, t)
    assert u_off % width == 0 and rows % L == 0
    ub = u_off // width
    bias_x = jnp.repeat(b_s.astype(F32).T, width // groups, axis=1)
    vec = pl.BlockSpec((1, width), lambda i: (0, 0))
    return pl.pallas_call(
        functools.partial(_gmlp_kernel, rows=rows, groups=groups), name="gmlp", grid=(t // rows,),
        in_specs=[pl.BlockSpec((rows, width), lambda i: (i, ub)),
                  pl.BlockSpec((rows, width), lambda i: (i, ub + 1)),
                  vec, vec,
                  pl.BlockSpec((groups, L, L), lambda i: (0, 0, 0)),
                  pl.BlockSpec((L, width), lambda i: (0, 0))],
        out_specs=pl.BlockSpec((rows, width), lambda i: (i, 0)),
        out_shape=jax.ShapeDtypeStruct((t, width), BF16),
        compiler_params=_cparams(("parallel",)))(
            proj, proj, ln_v[0].reshape(1, width).astype(F32), ln_v[1].reshape(1, width).astype(F32),
            w_s.astype(BF16), bias_x)


def _merge_kernel(h_ref, s_ref, a_ref, m_ref, wg0_ref, wg1_ref, wg2_ref, wos_ref, woa_ref, wom_ref, o_ref):
    h = h_ref[...]
    merged = (jax.nn.sigmoid(_dot(h, wg0_ref[...])) * _dot(s_ref[...], wos_ref[...])
              + jax.nn.sigmoid(_dot(h, wg1_ref[...])) * _dot(a_ref[...], woa_ref[...])
              + jax.nn.sigmoid(_dot(h, wg2_ref[...])) * _dot(m_ref[...], wom_ref[...]))
    o_ref[...] = merged.astype(o_ref.dtype)


def _merge(h, y_s, y_a, y_m, w_gate, w_os, w_oa, w_om):
    t, d = h.shape
    tm = min(MERGE_TM, t)
    tn = min(MERGE_TN, d)
    nj = d // tn
    act = lambda a: pl.BlockSpec((tm, a.shape[1]), lambda i, j: (i, 0))
    wcol = lambda w, off: pl.BlockSpec((w.shape[0], tn), lambda i, j: (0, j + off))
    return pl.pallas_call(
        _merge_kernel, name="merge", grid=(t // tm, nj),
        in_specs=[act(h), act(y_s), act(y_a), act(y_m),
                  wcol(w_gate, 0), wcol(w_gate, nj), wcol(w_gate, 2 * nj),
                  wcol(w_os, 0), wcol(w_oa, 0), wcol(w_om, 0)],
        out_specs=pl.BlockSpec((tm, tn), lambda i, j: (i, j)),
        out_shape=jax.ShapeDtypeStruct((t, d), BF16),
        compiler_params=_cparams(("parallel", "arbitrary")))(
            h, y_s, y_a, y_m, w_gate, w_gate, w_gate, w_os, w_oa, w_om)


def kernel(x_prompt, x_sample, rel_bias, norms, w_ffn1_in, w_ffn1_out, w_in, conv_w, conv_b, dt_bias, a_log,
           d_skip, ssm_norm, w_o_ssm, lambda_qk, diff_subln, w_o_diff, ln_v, w_spatial, b_spatial, w_o_gmlp,
           w_out, w_ffn2_in, w_ffn2_out):
    depth = norms.shape[0]
    d = x_prompt.shape[-1]
    d_inner = w_o_ssm.shape[1]
    conv_ch = conv_w.shape[2]
    heads = d_skip.shape[1]
    groups = (conv_ch - d_inner) // (2 * SSM_STATE)
    diff_w = w_o_diff.shape[1]
    diff_heads = diff_w // (2 * DIFF_HEAD_DIM)
    gmlp_w = w_o_gmlp.shape[1]
    pb, ps = x_prompt.shape[:2]
    sb, ss = x_sample.shape[:2]
    seq_lens = [ps] * pb + [ss] * sb
    n_prompt = pb * ps

    c_dt = d_inner + conv_ch
    c_diff = c_dt + 2 * heads
    c_gmlp = c_diff + 3 * diff_w
    c_gate = c_gmlp + 2 * gmlp_w
    q_off = c_dt
    u_off = q_off + 3 * diff_w
    n_main = u_off + 2 * gmlp_w
    col_scale = jnp.ones((1, n_main), F32).at[:, q_off:q_off + diff_w].set(DIFF_HEAD_DIM ** -0.5 * LOG2E)

    x = [x_prompt.reshape(n_prompt, d), x_sample.reshape(sb * ss, d)]
    h = _rmsnorm(x, norms[0, 0])
    for l in range(depth):
        n = norms[l]
        lambda_init = 0.8 - 0.6 * math.exp(-0.3 * l)
        wi = w_in[l]
        w_main = jnp.concatenate([wi[:, :c_dt], wi[:, c_diff:c_gate]], axis=1).astype(BF16)
        w_dt = jnp.pad(wi[:, c_dt:c_diff], ((0, 0), (0, LANES - 2 * heads))).astype(BF16)
        w_gate = wi[:, c_gate:].astype(BF16)

        y = _matmul(_matmul_swiglu(h, w_ffn1_in[l].astype(BF16)), w_ffn1_out[l].astype(BF16), BF16)
        x, h = _resid_norm(x, y, n[1], 0.5, n[2])
        x = [x]

        proj = _matmul(h, w_main, BF16, scale=col_scale)
        dt_raw = _matmul(h, w_dt, F32, tn=LANES)
        xbc = _conv_silu(proj, d_inner, conv_w[l], conv_b[l], seq_lens)
        y_ssm = _ssd(xbc, proj, dt_raw, dt_bias[l], a_log[l], d_skip[l], ssm_norm[l], seq_lens, groups)
        y_att = jnp.concatenate([
            _diff_attention(proj, q_off, diff_heads, rel_bias, lambda_qk[l], diff_subln[l], lambda_init, 0, pb, ps),
            _diff_attention(proj, q_off, diff_heads, rel_bias, lambda_qk[l], diff_subln[l], lambda_init,
                            n_prompt, sb, ss)], axis=0)
        y_gmlp = _gmlp(proj, u_off, gmlp_w, ln_v[l], w_spatial[l], b_spatial[l])
        merged = _merge(h, y_ssm, y_att, y_gmlp, w_gate, w_o_ssm[l].astype(BF16), w_o_diff[l].astype(BF16),
                        w_o_gmlp[l].astype(BF16))
        y = _matmul(merged, w_out[l].astype(BF16), BF16)
        x, h = _resid_norm(x, y, n[3], 1.0, n[4])
        x = [x]

        y = _matmul(_matmul_swiglu(h, w_ffn2_in[l].astype(BF16)), w_ffn2_out[l].astype(BF16), BF16)
        if l + 1 < depth:
            x, h = _resid_norm(x, y, n[5], 0.5, norms[l + 1, 0])
            x = [x]

    y_prompt = _resid_rows(x[0], y, norms[depth - 1, 5], 0.5, 0, n_prompt)
    y_sample = _resid_rows(x[0], y, norms[depth - 1, 5], 0.5, n_prompt, sb * ss)
    return y_prompt.reshape(x_prompt.shape), y_sample.reshape(x_sample.shape)
```

```python
import functools
import math

import jax
import jax.numpy as jnp
from jax import lax
from jax.experimental import pallas as pl
from jax.experimental.pallas import tpu as pltpu

F32 = jnp.float32
BF16 = jnp.bfloat16

SSM_HEAD_DIM = 64
SSM_STATE = 128
SSD_CHUNK = 128
DIFF_HEAD_DIM = 128
NUM_BUCKETS = 32
MAX_DISTANCE = 128
GMLP_CHUNK = 128
N_BRANCHES = 3
NORM_EPS = 1e-6
SUBLN_EPS = 1e-5
LN_EPS = 1e-5
LOG2E = math.log2(math.e)

LANES = 128
BF16_SUBLANES = 16
VMEM_LIMIT_BYTES = 56 * 1024 * 1024

MM_TM = 1024
MM_TN = 1024
ROW_TILE = 256
CONV_ROWS = 1024
CONV_COLS = 1024
CONV_BLOCK = 128
ATTN_TILE = 512
FAR_W = 4
MERGE_TM = 512
MERGE_TN = 256
GMLP_ROWS = 512


def _cparams(sem):
    return pltpu.CompilerParams(dimension_semantics=sem, vmem_limit_bytes=VMEM_LIMIT_BYTES)


def _dot(a, b):
    return jnp.dot(a, b, preferred_element_type=F32)


def _dot_nt(a, b):
    return lax.dot_general(a, b, (((1,), (1,)), ((), ())), preferred_element_type=F32)


def _dot_tn(a, b):
    return lax.dot_general(a, b, (((0,), (0,)), ((), ())), preferred_element_type=F32)


def _split_bf16(x, n):
    parts = []
    r = x
    for _ in range(n):
        p = r.astype(BF16)
        parts.append(p)
        r = r - p.astype(F32)
    return parts


def _rms(x, eps):
    return x * lax.rsqrt(jnp.mean(x * x, axis=-1, keepdims=True) + eps)


def _silu(x):
    return x * jax.nn.sigmoid(x)


def _softplus(x):
    return jnp.maximum(x, 0.0) + jnp.log1p(jnp.exp(-jnp.abs(x)))


def _mm_scale_kernel(a_ref, w_ref, s_ref, o_ref):
    o_ref[...] = (_dot(a_ref[...], w_ref[...]) * s_ref[...]).astype(o_ref.dtype)


def _mm_kernel(a_ref, w_ref, o_ref):
    o_ref[...] = _dot(a_ref[...], w_ref[...]).astype(o_ref.dtype)


def _mm_swiglu_kernel(a_ref, wg_ref, wu_ref, o_ref):
    a = a_ref[...]
    g = _dot(a, wg_ref[...])
    u = _dot(a, wu_ref[...])
    o_ref[...] = (_silu(g) * u).astype(o_ref.dtype)


def _matmul(a, w, out_dtype, scale=None, tm=None, tn=None):
    m, k = a.shape
    n = w.shape[1]
    tm = min(tm or MM_TM, m)
    tn = min(tn or MM_TN, n)
    in_specs = [pl.BlockSpec((tm, k), lambda i, j: (i, 0)), pl.BlockSpec((k, tn), lambda i, j: (0, j))]
    args = [a, w]
    body = _mm_kernel
    if scale is not None:
        in_specs.append(pl.BlockSpec((1, tn), lambda i, j: (0, j)))
        args.append(scale)
        body = _mm_scale_kernel
    return pl.pallas_call(
        body, name="mm", grid=(m // tm, n // tn), in_specs=in_specs,
        out_specs=pl.BlockSpec((tm, tn), lambda i, j: (i, j)),
        out_shape=jax.ShapeDtypeStruct((m, n), out_dtype),
        compiler_params=_cparams(("parallel", "arbitrary")))(*args)


def _matmul_swiglu(a, w):
    m, k = a.shape
    n = w.shape[1] // 2
    tm = min(MM_TM, m)
    tn = min(MM_TN // 2, n)
    nj = n // tn
    return pl.pallas_call(
        _mm_swiglu_kernel, name="mm_swiglu", grid=(m // tm, nj),
        in_specs=[pl.BlockSpec((tm, k), lambda i, j: (i, 0)),
                  pl.BlockSpec((k, tn), lambda i, j: (0, j)),
                  pl.BlockSpec((k, tn), lambda i, j: (0, j + nj))],
        out_specs=pl.BlockSpec((tm, tn), lambda i, j: (i, j)),
        out_shape=jax.ShapeDtypeStruct((m, n), BF16),
        compiler_params=_cparams(("parallel", "arbitrary")))(a, w, w)


def _stream_specs(parts, tr):
    d = parts[0].shape[1]
    if len(parts) == 1:
        return [pl.BlockSpec((tr, d), lambda i: (i, 0))], None
    na = parts[0].shape[0] // tr
    return [pl.BlockSpec((tr, d), lambda i: (jnp.minimum(i, na - 1), 0)),
            pl.BlockSpec((tr, d), lambda i: (jnp.maximum(i - na, 0), 0))], na


def _stream_block(x_refs, first_blocks):
    if len(x_refs) == 1:
        return x_refs[0][...]
    return jnp.where(pl.program_id(0) < first_blocks, x_refs[0][...], x_refs[1][...])


def _rmsnorm_kernel(*refs, n_parts, first_blocks):
    g_ref, o_ref = refs[n_parts:]
    o_ref[...] = (_rms(_stream_block(refs[:n_parts], first_blocks), NORM_EPS) * g_ref[...]).astype(o_ref.dtype)


def _rmsnorm(parts, g):
    d = parts[0].shape[1]
    t = sum(p.shape[0] for p in parts)
    tr = min(ROW_TILE, min(p.shape[0] for p in parts))
    x_specs, first_blocks = _stream_specs(parts, tr)
    return pl.pallas_call(
        functools.partial(_rmsnorm_kernel, n_parts=len(parts), first_blocks=first_blocks),
        name="rmsnorm", grid=(t // tr,),
        in_specs=x_specs + [pl.BlockSpec((1, d), lambda i: (0, 0))],
        out_specs=pl.BlockSpec((tr, d), lambda i: (i, 0)),
        out_shape=jax.ShapeDtypeStruct((t, d), BF16),
        compiler_params=_cparams(("parallel",)))(*parts, g.reshape(1, d))


def _resid_norm_kernel(*refs, n_parts, first_blocks, scale):
    y_ref, g1_ref, g2_ref, xo_ref, h_ref = refs[n_parts:]
    xn = _stream_block(refs[:n_parts], first_blocks) + scale * (_rms(y_ref[...].astype(F32), NORM_EPS) * g1_ref[...])
    xo_ref[...] = xn
    h_ref[...] = (_rms(xn, NORM_EPS) * g2_ref[...]).astype(h_ref.dtype)


def _resid_kernel(x_ref, y_ref, g1_ref, xo_ref, *, scale):
    xo_ref[...] = x_ref[...] + scale * (_rms(y_ref[...].astype(F32), NORM_EPS) * g1_ref[...])


def _resid_norm(parts, y, g_post, scale, g_next):
    t, d = y.shape
    tr = min(ROW_TILE, min(p.shape[0] for p in parts))
    x_specs, first_blocks = _stream_specs(parts, tr)
    row = pl.BlockSpec((tr, d), lambda i: (i, 0))
    vec = pl.BlockSpec((1, d), lambda i: (0, 0))
    return pl.pallas_call(
        functools.partial(_resid_norm_kernel, n_parts=len(parts), first_blocks=first_blocks, scale=scale),
        name="resid_norm", grid=(t // tr,),
        in_specs=x_specs + [row, vec, vec], out_specs=[row, row],
        out_shape=[jax.ShapeDtypeStruct((t, d), F32), jax.ShapeDtypeStruct((t, d), BF16)],
        compiler_params=_cparams(("parallel",)))(*parts, y, g_post.reshape(1, d), g_next.reshape(1, d))


def _resid_rows(x, y, g_post, scale, row_off, n_rows):
    d = x.shape[1]
    tr = min(ROW_TILE, n_rows)
    assert row_off % tr == 0 and n_rows % tr == 0
    b0 = row_off // tr
    row = pl.BlockSpec((tr, d), lambda i: (b0 + i, 0))
    return pl.pallas_call(
        functools.partial(_resid_kernel, scale=scale), name="resid", grid=(n_rows // tr,),
        in_specs=[row, row, pl.BlockSpec((1, d), lambda i: (0, 0))],
        out_specs=pl.BlockSpec((tr, d), lambda i: (i, 0)),
        out_shape=jax.ShapeDtypeStruct((n_rows, d), F32),
        compiler_params=_cparams(("parallel",)))(x, y, g_post.reshape(1, d))


def _conv_kernel(prev_ref, x_ref, next_ref, w_ref, b_ref, sh_ref, o_ref, *, rows, width, start_tiles, end_tiles):
    i = pl.program_id(0)
    halo = BF16_SUBLANES
    pad = width // 2
    blk = CONV_BLOCK
    is_start = functools.reduce(jnp.logical_or, [i == s for s in start_tiles])
    is_end = functools.reduce(jnp.logical_or, [i == e for e in end_tiles])
    prev = prev_ref[...]
    nxt = next_ref[...]
    ext = jnp.concatenate([jnp.where(is_start, jnp.zeros_like(prev), prev), x_ref[...],
                           jnp.where(is_end, jnp.zeros_like(nxt), nxt)], axis=0)
    w = w_ref[...]
    bias = b_ref[...]
    select = sh_ref[...]
    taps = [k for k in range(width) if k != pad]
    for rb in range(rows // blk):
        shifted = _dot(select, ext[rb * blk:rb * blk + blk + 2 * halo, :])
        acc = bias + w[pad:pad + 1, :] * x_ref[rb * blk:(rb + 1) * blk, :].astype(F32)
        for n, k in enumerate(taps):
            acc = acc + w[k:k + 1, :] * shifted[n * blk:(n + 1) * blk, :]
        o_ref[rb * blk:(rb + 1) * blk, :] = _silu(acc).astype(o_ref.dtype)


def _conv_silu(proj, col_off, conv_w, conv_b, seq_lens):
    t = proj.shape[0]
    width, ch = conv_w.shape
    rows = min(CONV_ROWS, min(seq_lens))
    cols = min(CONV_COLS, ch)
    halo = BF16_SUBLANES
    blk = CONV_BLOCK
    pad = width // 2
    assert col_off % cols == 0 and ch % cols == 0 and all(s % rows == 0 for s in seq_lens)
    assert rows % blk == 0 and pad <= halo
    starts, ends, pos = [], [], 0
    for s in seq_lens:
        starts.append(pos // rows)
        pos += s
        ends.append(pos // rows - 1)
    rb = rows // halo
    nhalo = t // halo
    cb = col_off // cols
    taps = jnp.array([k for k in range(width) if k != pad])
    src = jnp.arange(blk)[None, :] + halo + (taps - pad)[:, None]
    select = (src.reshape(-1)[:, None] == jnp.arange(blk + 2 * halo)[None, :]).astype(BF16)
    return pl.pallas_call(
        functools.partial(_conv_kernel, rows=rows, width=width, start_tiles=tuple(starts), end_tiles=tuple(ends)),
        name="conv_silu", grid=(t // rows, ch // cols),
        in_specs=[pl.BlockSpec((halo, cols), lambda i, j: (jnp.maximum(i * rb - 1, 0), cb + j)),
                  pl.BlockSpec((rows, cols), lambda i, j: (i, cb + j)),
                  pl.BlockSpec((halo, cols), lambda i, j: (jnp.minimum((i + 1) * rb, nhalo - 1), cb + j)),
                  pl.BlockSpec((width, cols), lambda i, j: (0, j)),
                  pl.BlockSpec((1, cols), lambda i, j: (0, j)),
                  pl.BlockSpec(select.shape, lambda i, j: (0, 0))],
        out_specs=pl.BlockSpec((rows, cols), lambda i, j: (i, j)),
        out_shape=jax.ShapeDtypeStruct((t, ch), BF16),
        compiler_params=_cparams(("parallel", "arbitrary")))(
            proj, proj, proj, conv_w, conv_b.reshape(1, ch), select)


def _ssd_kernel(*refs, reverse, n_chunks, reset_chunks, groups, heads, d_inner):
    if reverse:
        (xbc_ref, dt_ref, dtb_ref, alog_ref, tri_ref, trit_ref, exp_ref,
         yf_ref, z_ref, nw_ref, o_ref, state_ref) = refs
    else:
        (xbc_ref, dt_ref, dtb_ref, alog_ref, tri_ref, trit_ref, exp_ref,
         dskip_ref, o_ref, state_ref) = refs
    step = pl.program_id(0)
    chunk = (n_chunks - 1 - step) if reverse else step
    L = SSD_CHUNK
    gw = d_inner // groups
    hpg = heads // groups
    n = SSM_STATE
    col0 = heads if reverse else 0

    @pl.when(functools.reduce(jnp.logical_or, [chunk == c for c in reset_chunks]))
    def _():
        state_ref[...] = jnp.zeros_like(state_ref)

    dt = _softplus(dt_ref[...] + dtb_ref[...])
    dta = dt * (-jnp.exp(alog_ref[...]))
    dt_t = dt.T
    dta_t = dta.T
    tri = tri_ref[...]
    tri_t = trit_ref[...]
    acum = sum(_dot(tri, p) for p in _split_bf16(dta, 3))
    acum_t = sum(_dot(p, tri_t) for p in _split_bf16(dta_t, 3))
    edge = 0 if reverse else L - 1
    a_end = acum[edge:edge + 1, :]
    to_end = jnp.exp(a_end - acum) * dt
    ea = jnp.exp(acum)
    expand = exp_ref[...]
    to_end_x = sum(_dot(p, expand) for p in _split_bf16(to_end, 2))
    ea_x = sum(_dot(p, expand) for p in _split_bf16(ea, 2))
    decay_x = ea_x[edge:edge + 1, :]

    row = lax.broadcasted_iota(jnp.int32, (L, L), 0)
    colm = lax.broadcasted_iota(jnp.int32, (L, L), 1)
    causal = (row <= colm) if reverse else (row >= colm)
    lane = lax.broadcasted_iota(jnp.int32, (L, LANES), 1)
    first_half = lane < SSM_HEAD_DIM

    for g in range(groups):
        xs = xbc_ref[:, g * gw:(g + 1) * gw]
        bm = xbc_ref[:, d_inner + g * n:d_inner + (g + 1) * n]
        cm = xbc_ref[:, d_inner + groups * n + g * n:d_inner + groups * n + (g + 1) * n]
        cb = _dot_nt(cm, bm)
        st = state_ref[g]
        y_off = _dot(cm, st.astype(BF16)) * ea_x[:, g * gw:(g + 1) * gw]
        xs_scaled = (xs.astype(F32) * to_end_x[:, g * gw:(g + 1) * gw]).astype(BF16)
        state_ref[g] = st * decay_x[:, g * gw:(g + 1) * gw] + _dot_tn(bm, xs_scaled)
        pairs = []
        for k in range(hpg // 2):
            xp = xs[:, k * LANES:(k + 1) * LANES]
            y_pair = None
            for side in range(2):
                c = col0 + g * hpg + 2 * k + side
                seg = acum[:, c:c + 1] - acum_t[c:c + 1, :]
                decay = jnp.exp(jnp.where(causal, seg, -jnp.inf))
                m = (cb * decay * dt_t[c:c + 1, :]).astype(BF16)
                keep = first_half if side == 0 else jnp.logical_not(first_half)
                part = _dot(m, jnp.where(keep, xp, jnp.zeros_like(xp)))
                y_pair = part if y_pair is None else y_pair + part
            lo = g * gw + k * LANES
            y_pair = y_pair + y_off[:, k * LANES:(k + 1) * LANES]
            if not reverse:
                o_ref[:, lo:lo + LANES] = y_pair + dskip_ref[:, lo:lo + LANES] * xp.astype(F32)
            else:
                pairs.append(y_pair)
        if reverse:
            y = jnp.concatenate(pairs, axis=1) + yf_ref[:, g * gw:(g + 1) * gw]
            y = y * _silu(z_ref[:, g * gw:(g + 1) * gw].astype(F32))
            o_ref[:, g * gw:(g + 1) * gw] = (_rms(y, NORM_EPS) * nw_ref[:, g * gw:(g + 1) * gw]).astype(o_ref.dtype)


def _ssd(xbc, proj, dt_raw, dt_bias, a_log, d_skip, norm_w, seq_lens, groups):
    t = xbc.shape[0]
    heads = d_skip.shape[0]
    d_inner = heads * SSM_HEAD_DIM
    L = SSD_CHUNK
    nc = t // L
    assert 2 * heads <= LANES and (heads // groups) % 2 == 0
    starts, lasts, pos = [], [], 0
    for s in seq_lens:
        assert s % L == 0
        starts.append(pos // L)
        pos += s
        lasts.append(pos // L - 1)
    padh = LANES - 2 * heads
    dtb = jnp.pad(dt_bias.reshape(-1), (0, padh))
    alog = jnp.pad(a_log.reshape(-1), (0, padh))
    tri_lo = jnp.tril(jnp.ones((L, L), F32)).astype(BF16)
    tri_up = jnp.triu(jnp.ones((L, L), F32)).astype(BF16)
    head_of_col = jnp.arange(d_inner) // SSM_HEAD_DIM
    dskip_x = jnp.repeat(d_skip.astype(F32), SSM_HEAD_DIM).reshape(1, d_inner)

    def run(reverse, extra_args, extra_specs, out_dtype):
        cidx = (lambda i: nc - 1 - i) if reverse else (lambda i: i)
        tri = tri_up if reverse else tri_lo
        expand = (jnp.arange(LANES)[:, None] == (head_of_col + (heads if reverse else 0))[None, :]).astype(BF16)
        const = lambda shape: pl.BlockSpec(shape, lambda i: (0, 0))
        in_specs = [pl.BlockSpec((L, xbc.shape[1]), lambda i: (cidx(i), 0)),
                    pl.BlockSpec((L, LANES), lambda i: (cidx(i), 0)),
                    const((1, LANES)), const((1, LANES)),
                    const((L, L)), const((L, L)), const((LANES, d_inner))] + extra_specs(cidx)
        return pl.pallas_call(
            functools.partial(_ssd_kernel, reverse=reverse, n_chunks=nc,
                              reset_chunks=tuple(lasts if reverse else starts),
                              groups=groups, heads=heads, d_inner=d_inner),
            name="ssd_bwd" if reverse else "ssd_fwd", grid=(nc,), in_specs=in_specs,
            out_specs=pl.BlockSpec((L, d_inner), lambda i: (cidx(i), 0)),
            out_shape=jax.ShapeDtypeStruct((t, d_inner), out_dtype),
            scratch_shapes=[pltpu.VMEM((groups, SSM_STATE, d_inner // groups), F32)],
            compiler_params=_cparams(("arbitrary",)))(
                xbc, dt_raw, dtb.reshape(1, LANES), alog.reshape(1, LANES), tri, tri.T, expand, *extra_args)

    y_fwd = run(False, [dskip_x], lambda cidx: [pl.BlockSpec((1, d_inner), lambda i: (0, 0))], F32)
    return run(True, [y_fwd, proj, norm_w.reshape(1, d_inner)],
               lambda cidx: [pl.BlockSpec((L, d_inner), lambda i: (cidx(i), 0)),
                             pl.BlockSpec((L, d_inner), lambda i: (cidx(i), 0)),
                             pl.BlockSpec((1, d_inner), lambda i: (0, 0))], BF16)


def _t5_bucket(rel):
    half = NUM_BUCKETS // 2
    max_exact = half // 2
    n = jnp.abs(rel)
    log_ratio = jnp.log(jnp.maximum(n, 1).astype(F32) / max_exact) / math.log(MAX_DISTANCE / max_exact)
    large = jnp.minimum(max_exact + (log_ratio * (half - max_exact)).astype(jnp.int32), half - 1)
    return jnp.where(rel > 0, half, 0) + jnp.where(n < max_exact, n, large)


def _attn_kernel(far_ref, lq_ref, sw_ref, q_ref, k_ref, v_ref, bias_ref, o_ref, m_ref, l_ref, acc_ref,
                 sa_ref, sb_ref, *, t, nk, lambda_init):
    h = pl.program_id(1)
    i = pl.program_id(2)
    dh = DIFF_HEAD_DIM
    nblk = t // LANES
    c_before = far_ref[h, 0]
    c_after = far_ref[h, 1]
    m_ref[...] = jnp.full(m_ref.shape, -jnp.inf, F32)
    l_ref[...] = jnp.zeros(l_ref.shape, F32)
    acc_ref[...] = jnp.zeros(acc_ref.shape, F32)

    def qk(kb, j, width=1):
        ks = k_ref[pl.ds(pl.multiple_of(kb * t, t), width * t), j * dh:(j + 1) * dh]
        return _dot_nt(q_ref[:, j * dh:(j + 1) * dh], ks)

    def softmax_pv(kb, s_of, shift, width=1):
        vs = v_ref[pl.ds(pl.multiple_of(kb * t, t), width * t), :]
        for j in range(2):
            s = s_of(j)
            blocks = [s[:, c * LANES:(c + 1) * LANES] for c in range(width * nblk)]
            row_max = jnp.max(functools.reduce(jnp.maximum, blocks), axis=-1, keepdims=True)
            m_old = m_ref[j]
            m_new = jnp.maximum(m_old, row_max + shift)
            alpha = jnp.exp2(m_old - m_new)
            ref = m_new - shift
            ps = [jnp.exp2(b - ref) for b in blocks]
            l_ref[j] = alpha * l_ref[j] + functools.reduce(jnp.add, ps)
            p = jnp.concatenate([x.astype(BF16) for x in ps], axis=1)
            acc_ref[j] = jnp.concatenate([alpha] * (acc_ref.shape[2] // LANES), axis=1) * acc_ref[j] + _dot(p, vs)
            m_ref[j] = m_new

    def update(kb, bias_tile):
        softmax_pv(kb, lambda j: qk(kb, j) + bias_tile, 0.0)

    def far_loop(lo, hi, shift):
        w = FAR_W
        n = hi - lo
        nd = n // w
        base = lo + n % w

        def single(kb, carry):
            softmax_pv(kb, lambda j: qk(kb, j), shift)
            return carry

        lax.fori_loop(lo, base, single, 0)

        @pl.when(nd > 0)
        def _():
            for j in range(2):
                sa_ref[j] = qk(base, j, w)

            def pair(kp, carry):
                d0 = base + 2 * w * kp
                d1 = jnp.minimum(d0 + w, hi - w)
                d2 = jnp.minimum(d0 + 2 * w, hi - w)
                for j in range(2):
                    sb_ref[j] = qk(d1, j, w)
                softmax_pv(d0, lambda j: sa_ref[j], shift, w)
                for j in range(2):
                    sa_ref[j] = qk(d2, j, w)
                softmax_pv(d1, lambda j: sb_ref[j], shift, w)
                return carry

            lax.fori_loop(0, nd // 2, pair, 0)

            @pl.when(nd % 2 == 1)
            def _():
                softmax_pv(hi - w, lambda j: sa_ref[j], shift, w)

    far_loop(0, jnp.maximum(i - 1, 0), c_before)

    @pl.when(i >= 1)
    def _():
        update(i - 1, bias_ref[0, 0])

    update(i, bias_ref[0, 1])

    @pl.when(i + 1 < nk)
    def _():
        update(i + 1, bias_ref[0, 2])

    far_loop(jnp.minimum(i + 2, nk), nk, c_after)

    lq = lq_ref[...]
    lam = (jnp.exp(jnp.sum(lq[0:1] * lq[1:2], axis=-1, keepdims=True))
           - jnp.exp(jnp.sum(lq[2:3] * lq[3:4], axis=-1, keepdims=True)) + lambda_init)
    l0 = jnp.sum(l_ref[0], axis=-1, keepdims=True)
    l1 = jnp.sum(l_ref[1], axis=-1, keepdims=True)
    o = acc_ref[0] / l0 - lam * (acc_ref[1] / l1)
    o_ref[...] = (_rms(o, SUBLN_EPS) * sw_ref[...] * (1.0 - lambda_init)).astype(o_ref.dtype)


def _near_bias_tiles(table_t, t):
    x = jnp.concatenate([jnp.arange(t), jnp.arange(t) - t])
    rel = (jnp.arange(-1, 2) * t)[:, None] + x[None, :]
    w = table_t[:, _t5_bucket(rel)]
    flat = jnp.tile(w, (1, 1, t))[..., :t * (2 * t - 1)]
    return flat.reshape(w.shape[0], 3, t, 2 * t - 1)[..., :t]


def _diff_attention(proj, q_off, n_heads, rel_bias, lambda_qk, subln_w, lambda_init, tok_off, bsz, seq):
    t = min(ATTN_TILE, seq)
    nk = seq // t
    hw = 2 * DIFF_HEAD_DIM
    assert t >= MAX_DISTANCE and seq % t == 0 and tok_off % seq == 0 and q_off % hw == 0
    qb = q_off // hw
    table_t = rel_bias.astype(F32).T * LOG2E
    bias_near = _near_bias_tiles(table_t, t)
    far = table_t[:, _t5_bucket(jnp.array([-2 * MAX_DISTANCE, 2 * MAX_DISTANCE]))]
    row0 = tok_off // t
    seq0 = tok_off // seq
    return pl.pallas_call(
        functools.partial(_attn_kernel, t=t, nk=nk, lambda_init=lambda_init),
        name="diff_attn", grid=(bsz, n_heads, nk),
        in_specs=[pl.BlockSpec(memory_space=pltpu.SMEM),
                  pl.BlockSpec((4, DIFF_HEAD_DIM), lambda b, h, i: (0, 0)),
                  pl.BlockSpec((1, hw), lambda b, h, i: (0, 0)),
                  pl.BlockSpec((t, hw), lambda b, h, i: (row0 + b * nk + i, qb + h)),
                  pl.BlockSpec((seq, hw), lambda b, h, i: (seq0 + b, qb + n_heads + h),
                               pipeline_mode=pl.Buffered(1)),
                  pl.BlockSpec((seq, hw), lambda b, h, i: (seq0 + b, qb + 2 * n_heads + h),
                               pipeline_mode=pl.Buffered(1)),
                  pl.BlockSpec((1, 3, t, t), lambda b, h, i: (h, 0, 0, 0))],
        out_specs=pl.BlockSpec((t, hw), lambda b, h, i: (b * nk + i, h)),
        out_shape=jax.ShapeDtypeStruct((bsz * seq, n_heads * hw), BF16),
        scratch_shapes=[pltpu.VMEM((2, t, LANES), F32), pltpu.VMEM((2, t, LANES), F32),
                        pltpu.VMEM((2, t, hw), F32), pltpu.VMEM((2, t, FAR_W * t), F32),
                        pltpu.VMEM((2, t, FAR_W * t), F32)],
        compiler_params=_cparams(("parallel", "parallel", "arbitrary")))(
            far, lambda_qk.astype(F32), subln_w.reshape(1, hw).astype(F32), proj, proj, proj, bias_near)


def _gelu_tanh(x):
    return 0.5 * x * (1.0 + jnp.tanh(math.sqrt(2.0 / math.pi) * (x + 0.044715 * (x * x * x))))


def _gmlp_kernel(u_ref, v_ref, lng_ref, lnb_ref, ws_ref, bs_ref, o_ref, *, rows, groups):
    u = _gelu_tanh(u_ref[...].astype(F32))
    v = _gelu_tanh(v_ref[...].astype(F32))
    mu = jnp.mean(v, axis=-1, keepdims=True)
    vc = v - mu
    vn = (vc * lax.rsqrt(jnp.mean(vc * vc, axis=-1, keepdims=True) + LN_EPS) * lng_ref[...] + lnb_ref[...]).astype(BF16)
    gd = v.shape[1] // groups
    L = GMLP_CHUNK
    for c in range(rows // L):
        for g in range(groups):
            mixed = _dot(ws_ref[g], vn[c * L:(c + 1) * L, g * gd:(g + 1) * gd]) + bs_ref[:, g * gd:(g + 1) * gd]
            o_ref[c * L:(c + 1) * L, g * gd:(g + 1) * gd] = (
                u[c * L:(c + 1) * L, g * gd:(g + 1) * gd] * mixed).astype(o_ref.dtype)


def _gmlp(proj, u_off, width, ln_v, w_s, b_s):
    t = proj.shape[0]
    groups = w_s.shape[0]
    L = GMLP_CHUNK
    rows = min(GMLP_ROWS, t)
    assert u_off % width == 0 and rows % L == 0
    ub = u_off // width
    bias_x = jnp.repeat(b_s.astype(F32).T, width // groups, axis=1)
    vec = pl.BlockSpec((1, width), lambda i: (0, 0))
    return pl.pallas_call(
        functools.partial(_gmlp_kernel, rows=rows, groups=groups), name="gmlp", grid=(t // rows,),
        in_specs=[pl.BlockSpec((rows, width), lambda i: (i, ub)),
                  pl.BlockSpec((rows, width), lambda i: (i, ub + 1)),
                  vec, vec,
                  pl.BlockSpec((groups, L, L), lambda i: (0, 0, 0)),
                  pl.BlockSpec((L, width), lambda i: (0, 0))],
        out_specs=pl.BlockSpec((rows, width), lambda i: (i, 0)),
        out_shape=jax.ShapeDtypeStruct((t, width), BF16),
        compiler_params=_cparams(("parallel",)))(
            proj, proj, ln_v[0].reshape(1, width).astype(F32), ln_v[1].reshape(1, width).astype(F32),
            w_s.astype(BF16), bias_x)


def _merge_kernel(h_ref, s_ref, a0_ref, a1_ref, m_ref, wg0_ref, wg1_ref, wg2_ref, wos_ref, woa_ref, wom_ref,
                  o_ref, *, first_blocks):
    h = h_ref[...]
    y_a = jnp.where(pl.program_id(0) < first_blocks, a0_ref[...], a1_ref[...])
    merged = (jax.nn.sigmoid(_dot(h, wg0_ref[...])) * _dot(s_ref[...], wos_ref[...])
              + jax.nn.sigmoid(_dot(h, wg1_ref[...])) * _dot(y_a, woa_ref[...])
              + jax.nn.sigmoid(_dot(h, wg2_ref[...])) * _dot(m_ref[...], wom_ref[...]))
    o_ref[...] = merged.astype(o_ref.dtype)


def _merge(h, y_s, y_a_parts, y_m, w_gate, w_os, w_oa, w_om):
    t, d = h.shape
    ya0, ya1 = y_a_parts
    tm = min(MERGE_TM, ya0.shape[0], ya1.shape[0])
    tn = min(MERGE_TN, d)
    nj = d // tn
    na = ya0.shape[0] // tm
    assert ya0.shape[0] % tm == 0 and ya1.shape[0] % tm == 0
    act = lambda a: pl.BlockSpec((tm, a.shape[1]), lambda i, j: (i, 0))
    wcol = lambda w, off: pl.BlockSpec((w.shape[0], tn), lambda i, j: (0, j + off))
    return pl.pallas_call(
        functools.partial(_merge_kernel, first_blocks=na), name="merge", grid=(t // tm, nj),
        in_specs=[act(h), act(y_s),
                  pl.BlockSpec((tm, ya0.shape[1]), lambda i, j: (jnp.minimum(i, na - 1), 0)),
                  pl.BlockSpec((tm, ya1.shape[1]), lambda i, j: (jnp.maximum(i - na, 0), 0)),
                  act(y_m),
                  wcol(w_gate, 0), wcol(w_gate, nj), wcol(w_gate, 2 * nj),
                  wcol(w_os, 0), wcol(w_oa, 0), wcol(w_om, 0)],
        out_specs=pl.BlockSpec((tm, tn), lambda i, j: (i, j)),
        out_shape=jax.ShapeDtypeStruct((t, d), BF16),
        compiler_params=_cparams(("parallel", "arbitrary")))(
            h, y_s, ya0, ya1, y_m, w_gate, w_gate, w_gate, w_os, w_oa, w_om)


def kernel(x_prompt, x_sample, rel_bias, norms, w_ffn1_in, w_ffn1_out, w_in, conv_w, conv_b, dt_bias, a_log,
           d_skip, ssm_norm, w_o_ssm, lambda_qk, diff_subln, w_o_diff, ln_v, w_spatial, b_spatial, w_o_gmlp,
           w_out, w_ffn2_in, w_ffn2_out):
    depth = norms.shape[0]
    d = x_prompt.shape[-1]
    d_inner = w_o_ssm.shape[1]
    conv_ch = conv_w.shape[2]
    heads = d_skip.shape[1]
    groups = (conv_ch - d_inner) // (2 * SSM_STATE)
    diff_w = w_o_diff.shape[1]
    diff_heads = diff_w // (2 * DIFF_HEAD_DIM)
    gmlp_w = w_o_gmlp.shape[1]
    pb, ps = x_prompt.shape[:2]
    sb, ss = x_sample.shape[:2]
    seq_lens = [ps] * pb + [ss] * sb
    n_prompt = pb * ps

    c_dt = d_inner + conv_ch
    c_diff = c_dt + 2 * heads
    c_gmlp = c_diff + 3 * diff_w
    c_gate = c_gmlp + 2 * gmlp_w
    q_off = c_dt
    u_off = q_off + 3 * diff_w
    n_main = u_off + 2 * gmlp_w
    col_scale = jnp.ones((1, n_main), F32).at[:, q_off:q_off + diff_w].set(DIFF_HEAD_DIM ** -0.5 * LOG2E)

    x = [x_prompt.reshape(n_prompt, d), x_sample.reshape(sb * ss, d)]
    h = _rmsnorm(x, norms[0, 0])
    for l in range(depth):
        n = norms[l]
        lambda_init = 0.8 - 0.6 * math.exp(-0.3 * l)
        wi = w_in[l]
        w_main = jnp.concatenate([wi[:, :c_dt], wi[:, c_diff:c_gate]], axis=1).astype(BF16)
        w_dt = jnp.pad(wi[:, c_dt:c_diff], ((0, 0), (0, LANES - 2 * heads))).astype(BF16)
        w_gate = wi[:, c_gate:].astype(BF16)

        y = _matmul(_matmul_swiglu(h, w_ffn1_in[l].astype(BF16)), w_ffn1_out[l].astype(BF16), BF16)
        x, h = _resid_norm(x, y, n[1], 0.5, n[2])
        x = [x]

        proj = _matmul(h, w_main, BF16, scale=col_scale)
        dt_raw = _matmul(h, w_dt, F32, tn=LANES)
        xbc = _conv_silu(proj, d_inner, conv_w[l], conv_b[l], seq_lens)
        y_ssm = _ssd(xbc, proj, dt_raw, dt_bias[l], a_log[l], d_skip[l], ssm_norm[l], seq_lens, groups)
        y_att = [
            _diff_attention(proj, q_off, diff_heads, rel_bias, lambda_qk[l], diff_subln[l], lambda_init, 0, pb, ps),
            _diff_attention(proj, q_off, diff_heads, rel_bias, lambda_qk[l], diff_subln[l], lambda_init,
                            n_prompt, sb, ss)]
        y_gmlp = _gmlp(proj, u_off, gmlp_w, ln_v[l], w_spatial[l], b_spatial[l])
        merged = _merge(h, y_ssm, y_att, y_gmlp, w_gate, w_o_ssm[l].astype(BF16), w_o_diff[l].astype(BF16),
                        w_o_gmlp[l].astype(BF16))
        y = _matmul(merged, w_out[l].astype(BF16), BF16)
        x, h = _resid_norm(x, y, n[3], 1.0, n[4])
        x = [x]

        y = _matmul(_matmul_swiglu(h, w_ffn2_in[l].astype(BF16)), w_ffn2_out[l].astype(BF16), BF16)
        if l + 1 < depth:
            x, h = _resid_norm(x, y, n[5], 0.5, norms[l + 1, 0])
            x = [x]

    y_prompt = _resid_rows(x[0], y, norms[depth - 1, 5], 0.5, 0, n_prompt)
    y_sample = _resid_rows(x[0], y, norms[depth - 1, 5], 0.5, n_prompt, sb * ss)
    return y_prompt.reshape(x_prompt.shape), y_sample.reshape(x_sample.shape)
```

```python
import functools
import math

import jax
import jax.numpy as jnp
from jax import lax
from jax.experimental import pallas as pl
from jax.experimental.pallas import tpu as pltpu

F32 = jnp.float32
BF16 = jnp.bfloat16

SSM_HEAD_DIM = 64
SSM_STATE = 128
SSD_CHUNK = 128
DIFF_HEAD_DIM = 128
NUM_BUCKETS = 32
MAX_DISTANCE = 128
GMLP_CHUNK = 128
N_BRANCHES = 3
NORM_EPS = 1e-6
SUBLN_EPS = 1e-5
LN_EPS = 1e-5
LOG2E = math.log2(math.e)

LANES = 128
BF16_SUBLANES = 16
VMEM_LIMIT_BYTES = 56 * 1024 * 1024

MM_TM = 1024
MM_TN = 1024
ROW_TILE = 256
CONV_ROWS = 1024
CONV_COLS = 1024
CONV_BLOCK = 128
ATTN_TILE = 512
FAR_W = 4
MERGE_TM = 512
MERGE_TN = 256
GMLP_ROWS = 512


def _cparams(sem):
    return pltpu.CompilerParams(dimension_semantics=sem, vmem_limit_bytes=VMEM_LIMIT_BYTES)


def _dot(a, b):
    return jnp.dot(a, b, preferred_element_type=F32)


def _dot_nt(a, b):
    return lax.dot_general(a, b, (((1,), (1,)), ((), ())), preferred_element_type=F32)


def _dot_tn(a, b):
    return lax.dot_general(a, b, (((0,), (0,)), ((), ())), preferred_element_type=F32)


def _split_bf16(x, n):
    parts = []
    r = x
    for _ in range(n):
        p = r.astype(BF16)
        parts.append(p)
        r = r - p.astype(F32)
    return parts


def _rms(x, eps):
    return x * lax.rsqrt(jnp.mean(x * x, axis=-1, keepdims=True) + eps)


def _silu(x):
    return x * jax.nn.sigmoid(x)


def _softplus(x):
    return jnp.maximum(x, 0.0) + jnp.log1p(jnp.exp(-jnp.abs(x)))


def _mm_scale_kernel(a_ref, w_ref, s_ref, o_ref):
    o_ref[...] = (_dot(a_ref[...], w_ref[...]) * s_ref[...]).astype(o_ref.dtype)


def _mm_kernel(a_ref, w_ref, o_ref):
    o_ref[...] = _dot(a_ref[...], w_ref[...]).astype(o_ref.dtype)


def _mm_swiglu_kernel(a_ref, wg_ref, wu_ref, o_ref):
    a = a_ref[...]
    g = _dot(a, wg_ref[...])
    u = _dot(a, wu_ref[...])
    o_ref[...] = (_silu(g) * u).astype(o_ref.dtype)


def _matmul(a, w, out_dtype, scale=None, tm=None, tn=None):
    m, k = a.shape
    n = w.shape[1]
    tm = min(tm or MM_TM, m)
    tn = min(tn or MM_TN, n)
    in_specs = [pl.BlockSpec((tm, k), lambda i, j: (i, 0)), pl.BlockSpec((k, tn), lambda i, j: (0, j))]
    args = [a, w]
    body = _mm_kernel
    if scale is not None:
        in_specs.append(pl.BlockSpec((1, tn), lambda i, j: (0, j)))
        args.append(scale)
        body = _mm_scale_kernel
    return pl.pallas_call(
        body, name="mm", grid=(m // tm, n // tn), in_specs=in_specs,
        out_specs=pl.BlockSpec((tm, tn), lambda i, j: (i, j)),
        out_shape=jax.ShapeDtypeStruct((m, n), out_dtype),
        compiler_params=_cparams(("parallel", "arbitrary")))(*args)


def _matmul_swiglu(a, w):
    m, k = a.shape
    n = w.shape[1] // 2
    tm = min(MM_TM, m)
    tn = min(MM_TN // 2, n)
    nj = n // tn
    return pl.pallas_call(
        _mm_swiglu_kernel, name="mm_swiglu", grid=(m // tm, nj),
        in_specs=[pl.BlockSpec((tm, k), lambda i, j: (i, 0)),
                  pl.BlockSpec((k, tn), lambda i, j: (0, j)),
                  pl.BlockSpec((k, tn), lambda i, j: (0, j + nj))],
        out_specs=pl.BlockSpec((tm, tn), lambda i, j: (i, j)),
        out_shape=jax.ShapeDtypeStruct((m, n), BF16),
        compiler_params=_cparams(("parallel", "arbitrary")))(a, w, w)


def _stream_specs(parts, tr):
    d = parts[0].shape[1]
    if len(parts) == 1:
        return [pl.BlockSpec((tr, d), lambda i: (i, 0))], None
    na = parts[0].shape[0] // tr
    return [pl.BlockSpec((tr, d), lambda i: (jnp.minimum(i, na - 1), 0)),
            pl.BlockSpec((tr, d), lambda i: (jnp.maximum(i - na, 0), 0))], na


def _stream_block(x_refs, first_blocks):
    if len(x_refs) == 1:
        return x_refs[0][...]
    return jnp.where(pl.program_id(0) < first_blocks, x_refs[0][...], x_refs[1][...])


def _rmsnorm_kernel(*refs, n_parts, first_blocks):
    g_ref, o_ref = refs[n_parts:]
    o_ref[...] = (_rms(_stream_block(refs[:n_parts], first_blocks), NORM_EPS) * g_ref[...]).astype(o_ref.dtype)


def _rmsnorm(parts, g):
    d = parts[0].shape[1]
    t = sum(p.shape[0] for p in parts)
    tr = min(ROW_TILE, min(p.shape[0] for p in parts))
    x_specs, first_blocks = _stream_specs(parts, tr)
    return pl.pallas_call(
        functools.partial(_rmsnorm_kernel, n_parts=len(parts), first_blocks=first_blocks),
        name="rmsnorm", grid=(t // tr,),
        in_specs=x_specs + [pl.BlockSpec((1, d), lambda i: (0, 0))],
        out_specs=pl.BlockSpec((tr, d), lambda i: (i, 0)),
        out_shape=jax.ShapeDtypeStruct((t, d), BF16),
        compiler_params=_cparams(("parallel",)))(*parts, g.reshape(1, d))


def _resid_norm_kernel(*refs, n_parts, first_blocks, scale):
    y_ref, g1_ref, g2_ref, xo_ref, h_ref = refs[n_parts:]
    xn = _stream_block(refs[:n_parts], first_blocks) + scale * (_rms(y_ref[...].astype(F32), NORM_EPS) * g1_ref[...])
    xo_ref[...] = xn
    h_ref[...] = (_rms(xn, NORM_EPS) * g2_ref[...]).astype(h_ref.dtype)


def _resid_kernel(x_ref, y_ref, g1_ref, xo_ref, *, scale):
    xo_ref[...] = x_ref[...] + scale * (_rms(y_ref[...].astype(F32), NORM_EPS) * g1_ref[...])


def _resid_norm(parts, y, g_post, scale, g_next):
    t, d = y.shape
    tr = min(ROW_TILE, min(p.shape[0] for p in parts))
    x_specs, first_blocks = _stream_specs(parts, tr)
    row = pl.BlockSpec((tr, d), lambda i: (i, 0))
    vec = pl.BlockSpec((1, d), lambda i: (0, 0))
    return pl.pallas_call(
        functools.partial(_resid_norm_kernel, n_parts=len(parts), first_blocks=first_blocks, scale=scale),
        name="resid_norm", grid=(t // tr,),
        in_specs=x_specs + [row, vec, vec], out_specs=[row, row],
        out_shape=[jax.ShapeDtypeStruct((t, d), F32), jax.ShapeDtypeStruct((t, d), BF16)],
        compiler_params=_cparams(("parallel",)))(*parts, y, g_post.reshape(1, d), g_next.reshape(1, d))


def _resid_rows(x, y, g_post, scale, row_off, n_rows):
    d = x.shape[1]
    tr = min(ROW_TILE, n_rows)
    assert row_off % tr == 0 and n_rows % tr == 0
    b0 = row_off // tr
    row = pl.BlockSpec((tr, d), lambda i: (b0 + i, 0))
    return pl.pallas_call(
        functools.partial(_resid_kernel, scale=scale), name="resid", grid=(n_rows // tr,),
        in_specs=[row, row, pl.BlockSpec((1, d), lambda i: (0, 0))],
        out_specs=pl.BlockSpec((tr, d), lambda i: (i, 0)),
        out_shape=jax.ShapeDtypeStruct((n_rows, d), F32),
        compiler_params=_cparams(("parallel",)))(x, y, g_post.reshape(1, d))


def _conv_kernel(prev_ref, x_ref, next_ref, w_ref, b_ref, sh_ref, o_ref, *, rows, width, start_tiles, end_tiles):
    i = pl.program_id(0)
    halo = BF16_SUBLANES
    pad = width // 2
    blk = CONV_BLOCK
    is_start = functools.reduce(jnp.logical_or, [i == s for s in start_tiles])
    is_end = functools.reduce(jnp.logical_or, [i == e for e in end_tiles])
    prev = prev_ref[...]
    nxt = next_ref[...]
    ext = jnp.concatenate([jnp.where(is_start, jnp.zeros_like(prev), prev), x_ref[...],
                           jnp.where(is_end, jnp.zeros_like(nxt), nxt)], axis=0)
    w = w_ref[...]
    bias = b_ref[...]
    select = sh_ref[...]
    taps = [k for k in range(width) if k != pad]
    for rb in range(rows // blk):
        shifted = _dot(select, ext[rb * blk:rb * blk + blk + 2 * halo, :])
        acc = bias + w[pad:pad + 1, :] * x_ref[rb * blk:(rb + 1) * blk, :].astype(F32)
        for n, k in enumerate(taps):
            acc = acc + w[k:k + 1, :] * shifted[n * blk:(n + 1) * blk, :]
        o_ref[rb * blk:(rb + 1) * blk, :] = _silu(acc).astype(o_ref.dtype)


def _conv_silu(proj, col_off, conv_w, conv_b, seq_lens):
    t = proj.shape[0]
    width, ch = conv_w.shape
    rows = min(CONV_ROWS, min(seq_lens))
    cols = min(CONV_COLS, ch)
    halo = BF16_SUBLANES
    blk = CONV_BLOCK
    pad = width // 2
    assert col_off % cols == 0 and ch % cols == 0 and all(s % rows == 0 for s in seq_lens)
    assert rows % blk == 0 and pad <= halo
    starts, ends, pos = [], [], 0
    for s in seq_lens:
        starts.append(pos // rows)
        pos += s
        ends.append(pos // rows - 1)
    rb = rows // halo
    nhalo = t // halo
    cb = col_off // cols
    taps = jnp.array([k for k in range(width) if k != pad])
    src = jnp.arange(blk)[None, :] + halo + (taps - pad)[:, None]
    select = (src.reshape(-1)[:, None] == jnp.arange(blk + 2 * halo)[None, :]).astype(BF16)
    return pl.pallas_call(
        functools.partial(_conv_kernel, rows=rows, width=width, start_tiles=tuple(starts), end_tiles=tuple(ends)),
        name="conv_silu", grid=(t // rows, ch // cols),
        in_specs=[pl.BlockSpec((halo, cols), lambda i, j: (jnp.maximum(i * rb - 1, 0), cb + j)),
                  pl.BlockSpec((rows, cols), lambda i, j: (i, cb + j)),
                  pl.BlockSpec((halo, cols), lambda i, j: (jnp.minimum((i + 1) * rb, nhalo - 1), cb + j)),
                  pl.BlockSpec((width, cols), lambda i, j: (0, j)),
                  pl.BlockSpec((1, cols), lambda i, j: (0, j)),
                  pl.BlockSpec(select.shape, lambda i, j: (0, 0))],
        out_specs=pl.BlockSpec((rows, cols), lambda i, j: (i, j)),
        out_shape=jax.ShapeDtypeStruct((t, ch), BF16),
        compiler_params=_cparams(("parallel", "arbitrary")))(
            proj, proj, proj, conv_w, conv_b.reshape(1, ch), select)


def _ssd_kernel(*refs, reverse, n_chunks, reset_chunks, groups, heads, d_inner):
    if reverse:
        (xbc_ref, dt_ref, dtn_ref, dtb_ref, alog_ref, tri_ref, trit_ref, exp_ref,
         yf_ref, z_ref, nw_ref, o_ref, state_ref, *tables) = refs
    else:
        (xbc_ref, dt_ref, dtn_ref, dtb_ref, alog_ref, tri_ref, trit_ref, exp_ref,
         dskip_ref, o_ref, state_ref, *tables) = refs
    tables = (tables[:4], tables[4:])
    step = pl.program_id(0)
    chunk = (n_chunks - 1 - step) if reverse else step
    L = SSD_CHUNK
    gw = d_inner // groups
    hpg = heads // groups
    n = SSM_STATE
    col0 = heads if reverse else 0
    edge = 0 if reverse else L - 1

    def prepare(dt_block_ref, p):
        acum2_ref, rowt_ref, tex_ref, eax_ref = tables[p]
        dt = _softplus(dt_block_ref[...] + dtb_ref[...])
        dta = dt * (-jnp.exp(alog_ref[...]))
        acum = sum(_dot(tri_ref[...], q) for q in _split_bf16(dta, 3))
        acum_t = sum(_dot(q, trit_ref[...]) for q in _split_bf16(dta.T, 3))
        to_end = jnp.exp(acum[edge:edge + 1, :] - acum) * dt
        ea = jnp.exp(acum)
        expand = exp_ref[...]
        acum2_ref[...] = acum * LOG2E
        rowt_ref[...] = acum_t * LOG2E - jnp.log2(dt.T)
        tex_ref[...] = sum(_dot(q, expand) for q in _split_bf16(to_end, 2))
        eax_ref[...] = sum(_dot(q, expand) for q in _split_bf16(ea, 2))

    def scan_chunk(p):
        acum2_ref, rowt_ref, tex_ref, eax_ref = tables[p]
        row = lax.broadcasted_iota(jnp.int32, (L, L), 0)
        colm = lax.broadcasted_iota(jnp.int32, (L, L), 1)
        causal = (row <= colm) if reverse else (row >= colm)
        lane = lax.broadcasted_iota(jnp.int32, (L, LANES), 1)
        first_half = lane < SSM_HEAD_DIM
        for g in range(groups):
            cols = slice(g * gw, (g + 1) * gw)
            xs = xbc_ref[:, cols]
            bm = xbc_ref[:, d_inner + g * n:d_inner + (g + 1) * n]
            cm = xbc_ref[:, d_inner + groups * n + g * n:d_inner + groups * n + (g + 1) * n]
            cb = _dot_nt(cm, bm)
            st = state_ref[g]
            y_off = _dot(cm, st.astype(BF16)) * eax_ref[:, cols]
            xs_scaled = (xs.astype(F32) * tex_ref[:, cols]).astype(BF16)
            state_ref[g] = st * eax_ref[edge:edge + 1, cols] + _dot_tn(bm, xs_scaled)
            pairs = []
            for k in range(hpg // 2):
                xp = xs[:, k * LANES:(k + 1) * LANES]
                y_pair = None
                for side in range(2):
                    c = col0 + g * hpg + 2 * k + side
                    seg = acum2_ref[:, c:c + 1] - rowt_ref[c:c + 1, :]
                    m = (cb * jnp.exp2(jnp.where(causal, seg, -jnp.inf))).astype(BF16)
                    keep = first_half if side == 0 else jnp.logical_not(first_half)
                    part = _dot(m, jnp.where(keep, xp, jnp.zeros_like(xp)))
                    y_pair = part if y_pair is None else y_pair + part
                lo = g * gw + k * LANES
                y_pair = y_pair + y_off[:, k * LANES:(k + 1) * LANES]
                if not reverse:
                    o_ref[:, lo:lo + LANES] = y_pair + dskip_ref[:, lo:lo + LANES] * xp.astype(F32)
                else:
                    pairs.append(y_pair)
            if reverse:
                y = jnp.concatenate(pairs, axis=1) + yf_ref[:, cols]
                y = y * _silu(z_ref[:, cols].astype(F32))
                o_ref[:, cols] = (_rms(y, NORM_EPS) * nw_ref[:, cols]).astype(o_ref.dtype)

    @pl.when(functools.reduce(jnp.logical_or, [chunk == c for c in reset_chunks]))
    def _():
        state_ref[...] = jnp.zeros_like(state_ref)

    @pl.when(step == 0)
    def _():
        prepare(dt_ref, 0)

    for p in range(2):
        @pl.when(step % 2 == p)
        def _():
            scan_chunk(p)
            prepare(dtn_ref, 1 - p)


def _ssd(xbc, proj, dt_raw, dt_bias, a_log, d_skip, norm_w, seq_lens, groups):
    t = xbc.shape[0]
    heads = d_skip.shape[0]
    d_inner = heads * SSM_HEAD_DIM
    L = SSD_CHUNK
    nc = t // L
    assert 2 * heads <= LANES and (heads // groups) % 2 == 0
    starts, lasts, pos = [], [], 0
    for s in seq_lens:
        assert s % L == 0
        starts.append(pos // L)
        pos += s
        lasts.append(pos // L - 1)
    padh = LANES - 2 * heads
    dtb = jnp.pad(dt_bias.reshape(-1), (0, padh))
    alog = jnp.pad(a_log.reshape(-1), (0, padh))
    tri_lo = jnp.tril(jnp.ones((L, L), F32)).astype(BF16)
    tri_up = jnp.triu(jnp.ones((L, L), F32)).astype(BF16)
    head_of_col = jnp.arange(d_inner) // SSM_HEAD_DIM
    dskip_x = jnp.repeat(d_skip.astype(F32), SSM_HEAD_DIM).reshape(1, d_inner)

    def run(reverse, extra_args, extra_specs, out_dtype):
        cidx = (lambda i: nc - 1 - i) if reverse else (lambda i: i)
        tri = tri_up if reverse else tri_lo
        expand = (jnp.arange(LANES)[:, None] == (head_of_col + (heads if reverse else 0))[None, :]).astype(BF16)
        const = lambda shape: pl.BlockSpec(shape, lambda i: (0, 0))
        in_specs = [pl.BlockSpec((L, xbc.shape[1]), lambda i: (cidx(i), 0)),
                    pl.BlockSpec((L, LANES), lambda i: (cidx(i), 0)),
                    pl.BlockSpec((L, LANES), lambda i: (cidx(jnp.minimum(i + 1, nc - 1)), 0)),
                    const((1, LANES)), const((1, LANES)),
                    const((L, L)), const((L, L)), const((LANES, d_inner))] + extra_specs(cidx)
        return pl.pallas_call(
            functools.partial(_ssd_kernel, reverse=reverse, n_chunks=nc,
                              reset_chunks=tuple(lasts if reverse else starts),
                              groups=groups, heads=heads, d_inner=d_inner),
            name="ssd_bwd" if reverse else "ssd_fwd", grid=(nc,), in_specs=in_specs,
            out_specs=pl.BlockSpec((L, d_inner), lambda i: (cidx(i), 0)),
            out_shape=jax.ShapeDtypeStruct((t, d_inner), out_dtype),
            scratch_shapes=[pltpu.VMEM((groups, SSM_STATE, d_inner // groups), F32)] + 2 * [
                pltpu.VMEM((L, LANES), F32), pltpu.VMEM((LANES, L), F32),
                pltpu.VMEM((L, d_inner), F32), pltpu.VMEM((L, d_inner), F32)],
            compiler_params=_cparams(("arbitrary",)))(
                xbc, dt_raw, dt_raw, dtb.reshape(1, LANES), alog.reshape(1, LANES), tri, tri.T, expand,
                *extra_args)

    y_fwd = run(False, [dskip_x], lambda cidx: [pl.BlockSpec((1, d_inner), lambda i: (0, 0))], F32)
    return run(True, [y_fwd, proj, norm_w.reshape(1, d_inner)],
               lambda cidx: [pl.BlockSpec((L, d_inner), lambda i: (cidx(i), 0)),
                             pl.BlockSpec((L, d_inner), lambda i: (cidx(i), 0)),
                             pl.BlockSpec((1, d_inner), lambda i: (0, 0))], BF16)


def _t5_bucket(rel):
    half = NUM_BUCKETS // 2
    max_exact = half // 2
    n = jnp.abs(rel)
    log_ratio = jnp.log(jnp.maximum(n, 1).astype(F32) / max_exact) / math.log(MAX_DISTANCE / max_exact)
    large = jnp.minimum(max_exact + (log_ratio * (half - max_exact)).astype(jnp.int32), half - 1)
    return jnp.where(rel > 0, half, 0) + jnp.where(n < max_exact, n, large)


def _attn_kernel(far_ref, lq_ref, sw_ref, q_ref, k_ref, v_ref, bias_ref, o_ref, m_ref, l_ref, acc_ref,
                 sa_ref, sb_ref, *, t, nk, lambda_init):
    h = pl.program_id(1)
    i = pl.program_id(2)
    dh = DIFF_HEAD_DIM
    nblk = t // LANES
    c_before = far_ref[h, 0]
    c_after = far_ref[h, 1]
    m_ref[...] = jnp.full(m_ref.shape, -jnp.inf, F32)
    l_ref[...] = jnp.zeros(l_ref.shape, F32)
    acc_ref[...] = jnp.zeros(acc_ref.shape, F32)

    def qk(kb, j, width=1):
        ks = k_ref[pl.ds(pl.multiple_of(kb * t, t), width * t), j * dh:(j + 1) * dh]
        return _dot_nt(q_ref[:, j * dh:(j + 1) * dh], ks)

    def softmax_pv(kb, s_of, shift, width=1):
        vs = v_ref[pl.ds(pl.multiple_of(kb * t, t), width * t), :]
        for j in range(2):
            s = s_of(j)
            blocks = [s[:, c * LANES:(c + 1) * LANES] for c in range(width * nblk)]
            row_max = jnp.max(functools.reduce(jnp.maximum, blocks), axis=-1, keepdims=True)
            m_old = m_ref[j]
            m_new = jnp.maximum(m_old, row_max + shift)
            alpha = jnp.exp2(m_old - m_new)
            ref = m_new - shift
            ps = [jnp.exp2(b - ref) for b in blocks]
            l_ref[j] = alpha * l_ref[j] + functools.reduce(jnp.add, ps)
            p = jnp.concatenate([x.astype(BF16) for x in ps], axis=1)
            acc_ref[j] = jnp.concatenate([alpha] * (acc_ref.shape[2] // LANES), axis=1) * acc_ref[j] + _dot(p, vs)
            m_ref[j] = m_new

    def update(kb, bias_tile):
        softmax_pv(kb, lambda j: qk(kb, j) + bias_tile, 0.0)

    def far_loop(lo, hi, shift):
        w = FAR_W
        n = hi - lo
        nd = n // w
        base = lo + n % w

        def single(kb, carry):
            softmax_pv(kb, lambda j: qk(kb, j), shift)
            return carry

        lax.fori_loop(lo, base, single, 0)

        @pl.when(nd > 0)
        def _():
            for j in range(2):
                sa_ref[j] = qk(base, j, w)

            def pair(kp, carry):
                d0 = base + 2 * w * kp
                d1 = jnp.minimum(d0 + w, hi - w)
                d2 = jnp.minimum(d0 + 2 * w, hi - w)
                for j in range(2):
                    sb_ref[j] = qk(d1, j, w)
                softmax_pv(d0, lambda j: sa_ref[j], shift, w)
                for j in range(2):
                    sa_ref[j] = qk(d2, j, w)
                softmax_pv(d1, lambda j: sb_ref[j], shift, w)
                return carry

            lax.fori_loop(0, nd // 2, pair, 0)

            @pl.when(nd % 2 == 1)
            def _():
                softmax_pv(hi - w, lambda j: sa_ref[j], shift, w)

    far_loop(0, jnp.maximum(i - 1, 0), c_before)

    @pl.when(i >= 1)
    def _():
        update(i - 1, bias_ref[0, 0])

    update(i, bias_ref[0, 1])

    @pl.when(i + 1 < nk)
    def _():
        update(i + 1, bias_ref[0, 2])

    far_loop(jnp.minimum(i + 2, nk), nk, c_after)

    lq = lq_ref[...]
    lam = (jnp.exp(jnp.sum(lq[0:1] * lq[1:2], axis=-1, keepdims=True))
           - jnp.exp(jnp.sum(lq[2:3] * lq[3:4], axis=-1, keepdims=True)) + lambda_init)
    l0 = jnp.sum(l_ref[0], axis=-1, keepdims=True)
    l1 = jnp.sum(l_ref[1], axis=-1, keepdims=True)
    o = acc_ref[0] / l0 - lam * (acc_ref[1] / l1)
    o_ref[...] = (_rms(o, SUBLN_EPS) * sw_ref[...] * (1.0 - lambda_init)).astype(o_ref.dtype)


def _near_bias_tiles(table_t, t):
    x = jnp.concatenate([jnp.arange(t), jnp.arange(t) - t])
    rel = (jnp.arange(-1, 2) * t)[:, None] + x[None, :]
    w = table_t[:, _t5_bucket(rel)]
    flat = jnp.tile(w, (1, 1, t))[..., :t * (2 * t - 1)]
    return flat.reshape(w.shape[0], 3, t, 2 * t - 1)[..., :t]


def _diff_attention(proj, q_off, n_heads, rel_bias, lambda_qk, subln_w, lambda_init, tok_off, bsz, seq):
    t = min(ATTN_TILE, seq)
    nk = seq // t
    hw = 2 * DIFF_HEAD_DIM
    assert t >= MAX_DISTANCE and seq % t == 0 and tok_off % seq == 0 and q_off % hw == 0
    qb = q_off // hw
    table_t = rel_bias.astype(F32).T * LOG2E
    bias_near = _near_bias_tiles(table_t, t)
    far = table_t[:, _t5_bucket(jnp.array([-2 * MAX_DISTANCE, 2 * MAX_DISTANCE]))]
    row0 = tok_off // t
    seq0 = tok_off // seq
    return pl.pallas_call(
        functools.partial(_attn_kernel, t=t, nk=nk, lambda_init=lambda_init),
        name="diff_attn", grid=(bsz, n_heads, nk),
        in_specs=[pl.BlockSpec(memory_space=pltpu.SMEM),
                  pl.BlockSpec((4, DIFF_HEAD_DIM), lambda b, h, i: (0, 0)),
                  pl.BlockSpec((1, hw), lambda b, h, i: (0, 0)),
                  pl.BlockSpec((t, hw), lambda b, h, i: (row0 + b * nk + i, qb + h)),
                  pl.BlockSpec((seq, hw), lambda b, h, i: (seq0 + b, qb + n_heads + h),
                               pipeline_mode=pl.Buffered(1)),
                  pl.BlockSpec((seq, hw), lambda b, h, i: (seq0 + b, qb + 2 * n_heads + h),
                               pipeline_mode=pl.Buffered(1)),
                  pl.BlockSpec((1, 3, t, t), lambda b, h, i: (h, 0, 0, 0))],
        out_specs=pl.BlockSpec((t, hw), lambda b, h, i: (b * nk + i, h)),
        out_shape=jax.ShapeDtypeStruct((bsz * seq, n_heads * hw), BF16),
        scratch_shapes=[pltpu.VMEM((2, t, LANES), F32), pltpu.VMEM((2, t, LANES), F32),
                        pltpu.VMEM((2, t, hw), F32), pltpu.VMEM((2, t, FAR_W * t), F32),
                        pltpu.VMEM((2, t, FAR_W * t), F32)],
        compiler_params=_cparams(("parallel", "parallel", "arbitrary")))(
            far, lambda_qk.astype(F32), subln_w.reshape(1, hw).astype(F32), proj, proj, proj, bias_near)


def _gelu_tanh(x):
    return 0.5 * x * (1.0 + jnp.tanh(math.sqrt(2.0 / math.pi) * (x + 0.044715 * (x * x * x))))


def _gmlp_kernel(u_ref, v_ref, lng_ref, lnb_ref, ws_ref, bs_ref, o_ref, *, rows, groups):
    u = _gelu_tanh(u_ref[...].astype(F32))
    v = _gelu_tanh(v_ref[...].astype(F32))
    mu = jnp.mean(v, axis=-1, keepdims=True)
    vc = v - mu
    vn = (vc * lax.rsqrt(jnp.mean(vc * vc, axis=-1, keepdims=True) + LN_EPS) * lng_ref[...] + lnb_ref[...]).astype(BF16)
    gd = v.shape[1] // groups
    L = GMLP_CHUNK
    for c in range(rows // L):
        for g in range(groups):
            mixed = _dot(ws_ref[g], vn[c * L:(c + 1) * L, g * gd:(g + 1) * gd]) + bs_ref[:, g * gd:(g + 1) * gd]
            o_ref[c * L:(c + 1) * L, g * gd:(g + 1) * gd] = (
                u[c * L:(c + 1) * L, g * gd:(g + 1) * gd] * mixed).astype(o_ref.dtype)


def _gmlp(proj, u_off, width, ln_v, w_s, b_s):
    t = proj.shape[0]
    groups = w_s.shape[0]
    L = GMLP_CHUNK
    rows = min(GMLP_ROWS, t)
    assert u_off % width == 0 and rows % L == 0
    ub = u_off // width
    bias_x = jnp.repeat(b_s.astype(F32).T, width // groups, axis=1)
    vec = pl.BlockSpec((1, width), lambda i: (0, 0))
    return pl.pallas_call(
        functools.partial(_gmlp_kernel, rows=rows, groups=groups), name="gmlp", grid=(t // rows,),
        in_specs=[pl.BlockSpec((rows, width), lambda i: (i, ub)),
                  pl.BlockSpec((rows, width), lambda i: (i, ub + 1)),
                  vec, vec,
                  pl.BlockSpec((groups, L, L), lambda i: (0, 0, 0)),
                  pl.BlockSpec((L, width), lambda i: (0, 0))],
        out_specs=pl.BlockSpec((rows, width), lambda i: (i, 0)),
        out_shape=jax.ShapeDtypeStruct((t, width), BF16),
        compiler_params=_cparams(("parallel",)))(
            proj, proj, ln_v[0].reshape(1, width).astype(F32), ln_v[1].reshape(1, width).astype(F32),
            w_s.astype(BF16), bias_x)


def _merge_kernel(h_ref, s_ref, a0_ref, a1_ref, m_ref, wg0_ref, wg1_ref, wg2_ref, wos_ref, woa_ref, wom_ref,
                  o_ref, *, first_blocks):
    h = h_ref[...]
    y_a = jnp.where(pl.program_id(0) < first_blocks, a0_ref[...], a1_ref[...])
    merged = (jax.nn.sigmoid(_dot(h, wg0_ref[...])) * _dot(s_ref[...], wos_ref[...])
              + jax.nn.sigmoid(_dot(h, wg1_ref[...])) * _dot(y_a, woa_ref[...])
              + jax.nn.sigmoid(_dot(h, wg2_ref[...])) * _dot(m_ref[...], wom_ref[...]))
    o_ref[...] = merged.astype(o_ref.dtype)


def _merge(h, y_s, y_a_parts, y_m, w_gate, w_os, w_oa, w_om):
    t, d = h.shape
    ya0, ya1 = y_a_parts
    tm = min(MERGE_TM, ya0.shape[0], ya1.shape[0])
    tn = min(MERGE_TN, d)
    nj = d // tn
    na = ya0.shape[0] // tm
    assert ya0.shape[0] % tm == 0 and ya1.shape[0] % tm == 0
    act = lambda a: pl.BlockSpec((tm, a.shape[1]), lambda i, j: (i, 0))
    wcol = lambda w, off: pl.BlockSpec((w.shape[0], tn), lambda i, j: (0, j + off))
    return pl.pallas_call(
        functools.partial(_merge_kernel, first_blocks=na), name="merge", grid=(t // tm, nj),
        in_specs=[act(h), act(y_s),
                  pl.BlockSpec((tm, ya0.shape[1]), lambda i, j: (jnp.minimum(i, na - 1), 0)),
                  pl.BlockSpec((tm, ya1.shape[1]), lambda i, j: (jnp.maximum(i - na, 0), 0)),
                  act(y_m),
                  wcol(w_gate, 0), wcol(w_gate, nj), wcol(w_gate, 2 * nj),
                  wcol(w_os, 0), wcol(w_oa, 0), wcol(w_om, 0)],
        out_specs=pl.BlockSpec((tm, tn), lambda i, j: (i, j)),
        out_shape=jax.ShapeDtypeStruct((t, d), BF16),
        compiler_params=_cparams(("parallel", "arbitrary")))(
            h, y_s, ya0, ya1, y_m, w_gate, w_gate, w_gate, w_os, w_oa, w_om)


def kernel(x_prompt, x_sample, rel_bias, norms, w_ffn1_in, w_ffn1_out, w_in, conv_w, conv_b, dt_bias, a_log,
           d_skip, ssm_norm, w_o_ssm, lambda_qk, diff_subln, w_o_diff, ln_v, w_spatial, b_spatial, w_o_gmlp,
           w_out, w_ffn2_in, w_ffn2_out):
    depth = norms.shape[0]
    d = x_prompt.shape[-1]
    d_inner = w_o_ssm.shape[1]
    conv_ch = conv_w.shape[2]
    heads = d_skip.shape[1]
    groups = (conv_ch - d_inner) // (2 * SSM_STATE)
    diff_w = w_o_diff.shape[1]
    diff_heads = diff_w // (2 * DIFF_HEAD_DIM)
    gmlp_w = w_o_gmlp.shape[1]
    pb, ps = x_prompt.shape[:2]
    sb, ss = x_sample.shape[:2]
    seq_lens = [ps] * pb + [ss] * sb
    n_prompt = pb * ps

    c_dt = d_inner + conv_ch
    c_diff = c_dt + 2 * heads
    c_gmlp = c_diff + 3 * diff_w
    c_gate = c_gmlp + 2 * gmlp_w
    q_off = c_dt
    u_off = q_off + 3 * diff_w
    n_main = u_off + 2 * gmlp_w
    col_scale = jnp.ones((1, n_main), F32).at[:, q_off:q_off + diff_w].set(DIFF_HEAD_DIM ** -0.5 * LOG2E)

    x = [x_prompt.reshape(n_prompt, d), x_sample.reshape(sb * ss, d)]
    h = _rmsnorm(x, norms[0, 0])
    for l in range(depth):
        n = norms[l]
        lambda_init = 0.8 - 0.6 * math.exp(-0.3 * l)
        wi = w_in[l]
        w_main = jnp.concatenate([wi[:, :c_dt], wi[:, c_diff:c_gate]], axis=1).astype(BF16)
        w_dt = jnp.pad(wi[:, c_dt:c_diff], ((0, 0), (0, LANES - 2 * heads))).astype(BF16)
        w_gate = wi[:, c_gate:].astype(BF16)

        y = _matmul(_matmul_swiglu(h, w_ffn1_in[l].astype(BF16)), w_ffn1_out[l].astype(BF16), BF16)
        x, h = _resid_norm(x, y, n[1], 0.5, n[2])
        x = [x]

        proj = _matmul(h, w_main, BF16, scale=col_scale)
        dt_raw = _matmul(h, w_dt, F32, tn=LANES)
        xbc = _conv_silu(proj, d_inner, conv_w[l], conv_b[l], seq_lens)
        y_ssm = _ssd(xbc, proj, dt_raw, dt_bias[l], a_log[l], d_skip[l], ssm_norm[l], seq_lens, groups)
        y_att = [
            _diff_attention(proj, q_off, diff_heads, rel_bias, lambda_qk[l], diff_subln[l], lambda_init, 0, pb, ps),
            _diff_attention(proj, q_off, diff_heads, rel_bias, lambda_qk[l], diff_subln[l], lambda_init,
                            n_prompt, sb, ss)]
        y_gmlp = _gmlp(proj, u_off, gmlp_w, ln_v[l], w_spatial[l], b_spatial[l])
        merged = _merge(h, y_ssm, y_att, y_gmlp, w_gate, w_o_ssm[l].astype(BF16), w_o_diff[l].astype(BF16),
                        w_o_gmlp[l].astype(BF16))
        y = _matmul(merged, w_out[l].astype(BF16), BF16)
        x, h = _resid_norm(x, y, n[3], 1.0, n[4])
        x = [x]

        y = _matmul(_matmul_swiglu(h, w_ffn2_in[l].astype(BF16)), w_ffn2_out[l].astype(BF16), BF16)
        if l + 1 < depth:
            x, h = _resid_norm(x, y, n[5], 0.5, norms[l + 1, 0])
            x = [x]

    y_prompt = _resid_rows(x[0], y, norms[depth - 1, 5], 0.5, 0, n_prompt)
    y_sample = _resid_rows(x[0], y, norms[depth - 1, 5], 0.5, n_prompt, sb * ss)
    return y_prompt.reshape(x_prompt.shape), y_sample.reshape(x_sample.shape)
```

```python
import functools
import math

import jax
import jax.numpy as jnp
from jax import lax
from jax.experimental import pallas as pl
from jax.experimental.pallas import tpu as pltpu

F32 = jnp.float32
BF16 = jnp.bfloat16

SSM_HEAD_DIM = 64
SSM_STATE = 128
SSD_CHUNK = 128
DIFF_HEAD_DIM = 128
NUM_BUCKETS = 32
MAX_DISTANCE = 128
GMLP_CHUNK = 128
NORM_EPS = 1e-6
SUBLN_EPS = 1e-5
LN_EPS = 1e-5
LOG2E = math.log2(math.e)

LANES = 128
BF16_SUBLANES = 16
VMEM_LIMIT_BYTES = 56 * 1024 * 1024

MM_TM = 1024
MM_TN = 1024
FFN_OUT_TN = 2048
ROW_TILE = 256
CONV_ROWS = 1024
CONV_COLS = 1024
CONV_BLOCK = 128
ATTN_TILE = 512
FAR_W = 4
MERGE_TM = 512
MERGE_TN = 256
GMLP_ROWS = 512


def _cparams(sem):
    return pltpu.CompilerParams(dimension_semantics=sem, vmem_limit_bytes=VMEM_LIMIT_BYTES)


def _dot(a, b):
    return jnp.dot(a, b, preferred_element_type=F32)


def _dot_nt(a, b):
    return lax.dot_general(a, b, (((1,), (1,)), ((), ())), preferred_element_type=F32)


def _dot_tn(a, b):
    return lax.dot_general(a, b, (((0,), (0,)), ((), ())), preferred_element_type=F32)


def _split_bf16(x, n):
    parts = []
    r = x
    for _ in range(n):
        p = r.astype(BF16)
        parts.append(p)
        r = r - p.astype(F32)
    return parts


def _rms(x, eps):
    return x * lax.rsqrt(jnp.mean(x * x, axis=-1, keepdims=True) + eps)


def _silu(x):
    return x * jax.nn.sigmoid(x)


def _softplus(x):
    return jnp.maximum(x, 0.0) + jnp.log1p(jnp.exp(-jnp.abs(x)))


def _mm_scale_kernel(a_ref, w_ref, s_ref, o_ref):
    o_ref[...] = (_dot(a_ref[...], w_ref[...]) * s_ref[...]).astype(o_ref.dtype)


def _mm_kernel(a_ref, w_ref, o_ref):
    o_ref[...] = _dot(a_ref[...], w_ref[...]).astype(o_ref.dtype)


def _mm_swiglu_kernel(a_ref, wg_ref, wu_ref, o_ref):
    a = a_ref[...]
    g = _dot(a, wg_ref[...])
    u = _dot(a, wu_ref[...])
    o_ref[...] = (_silu(g) * u).astype(o_ref.dtype)


def _matmul(a, w, out_dtype, scale=None, tm=None, tn=None):
    m, k = a.shape
    n = w.shape[1]
    tm = min(tm or MM_TM, m)
    tn = min(tn or MM_TN, n)
    in_specs = [pl.BlockSpec((tm, k), lambda i, j: (i, 0)), pl.BlockSpec((k, tn), lambda i, j: (0, j))]
    args = [a, w]
    body = _mm_kernel
    if scale is not None:
        in_specs.append(pl.BlockSpec((1, tn), lambda i, j: (0, j)))
        args.append(scale)
        body = _mm_scale_kernel
    return pl.pallas_call(
        body, name="mm", grid=(m // tm, n // tn), in_specs=in_specs,
        out_specs=pl.BlockSpec((tm, tn), lambda i, j: (i, j)),
        out_shape=jax.ShapeDtypeStruct((m, n), out_dtype),
        compiler_params=_cparams(("parallel", "arbitrary")))(*args)


def _matmul_swiglu(a, w):
    m, k = a.shape
    n = w.shape[1] // 2
    tm = min(MM_TM, m)
    tn = min(MM_TN // 2, n)
    nj = n // tn
    return pl.pallas_call(
        _mm_swiglu_kernel, name="mm_swiglu", grid=(m // tm, nj),
        in_specs=[pl.BlockSpec((tm, k), lambda i, j: (i, 0)),
                  pl.BlockSpec((k, tn), lambda i, j: (0, j)),
                  pl.BlockSpec((k, tn), lambda i, j: (0, j + nj))],
        out_specs=pl.BlockSpec((tm, tn), lambda i, j: (i, j)),
        out_shape=jax.ShapeDtypeStruct((m, n), BF16),
        compiler_params=_cparams(("parallel", "arbitrary")))(a, w, w)


def _stream_specs(parts, tr):
    d = parts[0].shape[1]
    if len(parts) == 1:
        return [pl.BlockSpec((tr, d), lambda i: (i, 0))], None
    na = parts[0].shape[0] // tr
    return [pl.BlockSpec((tr, d), lambda i: (jnp.minimum(i, na - 1), 0)),
            pl.BlockSpec((tr, d), lambda i: (jnp.maximum(i - na, 0), 0))], na


def _stream_block(x_refs, first_blocks):
    if len(x_refs) == 1:
        return x_refs[0][...]
    return jnp.where(pl.program_id(0) < first_blocks, x_refs[0][...], x_refs[1][...])


def _rmsnorm_kernel(*refs, n_parts, first_blocks):
    g_ref, o_ref = refs[n_parts:]
    o_ref[...] = (_rms(_stream_block(refs[:n_parts], first_blocks), NORM_EPS) * g_ref[...]).astype(o_ref.dtype)


def _rmsnorm(parts, g):
    d = parts[0].shape[1]
    t = sum(p.shape[0] for p in parts)
    tr = min(ROW_TILE, min(p.shape[0] for p in parts))
    x_specs, first_blocks = _stream_specs(parts, tr)
    return pl.pallas_call(
        functools.partial(_rmsnorm_kernel, n_parts=len(parts), first_blocks=first_blocks),
        name="rmsnorm", grid=(t // tr,),
        in_specs=x_specs + [pl.BlockSpec((1, d), lambda i: (0, 0))],
        out_specs=pl.BlockSpec((tr, d), lambda i: (i, 0)),
        out_shape=jax.ShapeDtypeStruct((t, d), BF16),
        compiler_params=_cparams(("parallel",)))(*parts, g.reshape(1, d))


def _resid_norm_kernel(*refs, n_parts, first_blocks, scale):
    y_ref, g1_ref, g2_ref, xo_ref, h_ref = refs[n_parts:]
    xn = _stream_block(refs[:n_parts], first_blocks) + scale * (_rms(y_ref[...].astype(F32), NORM_EPS) * g1_ref[...])
    xo_ref[...] = xn
    h_ref[...] = (_rms(xn, NORM_EPS) * g2_ref[...]).astype(h_ref.dtype)


def _resid_kernel(x_ref, y_ref, g1_ref, xo_ref, *, scale):
    xo_ref[...] = x_ref[...] + scale * (_rms(y_ref[...].astype(F32), NORM_EPS) * g1_ref[...])


def _resid_norm(parts, y, g_post, scale, g_next):
    t, d = y.shape
    tr = min(ROW_TILE, min(p.shape[0] for p in parts))
    x_specs, first_blocks = _stream_specs(parts, tr)
    row = pl.BlockSpec((tr, d), lambda i: (i, 0))
    vec = pl.BlockSpec((1, d), lambda i: (0, 0))
    return pl.pallas_call(
        functools.partial(_resid_norm_kernel, n_parts=len(parts), first_blocks=first_blocks, scale=scale),
        name="resid_norm", grid=(t // tr,),
        in_specs=x_specs + [row, vec, vec], out_specs=[row, row],
        out_shape=[jax.ShapeDtypeStruct((t, d), F32), jax.ShapeDtypeStruct((t, d), BF16)],
        compiler_params=_cparams(("parallel",)))(*parts, y, g_post.reshape(1, d), g_next.reshape(1, d))


def _resid_rows(x, y, g_post, scale, row_off, n_rows):
    d = x.shape[1]
    tr = min(ROW_TILE, n_rows)
    assert row_off % tr == 0 and n_rows % tr == 0
    b0 = row_off // tr
    row = pl.BlockSpec((tr, d), lambda i: (b0 + i, 0))
    return pl.pallas_call(
        functools.partial(_resid_kernel, scale=scale), name="resid", grid=(n_rows // tr,),
        in_specs=[row, row, pl.BlockSpec((1, d), lambda i: (0, 0))],
        out_specs=pl.BlockSpec((tr, d), lambda i: (i, 0)),
        out_shape=jax.ShapeDtypeStruct((n_rows, d), F32),
        compiler_params=_cparams(("parallel",)))(x, y, g_post.reshape(1, d))


def _conv_kernel(prev_ref, x_ref, next_ref, w_ref, b_ref, sh_ref, o_ref, *, rows, width, start_tiles, end_tiles):
    i = pl.program_id(0)
    halo = BF16_SUBLANES
    pad = width // 2
    blk = CONV_BLOCK
    is_start = functools.reduce(jnp.logical_or, [i == s for s in start_tiles])
    is_end = functools.reduce(jnp.logical_or, [i == e for e in end_tiles])
    prev = prev_ref[...]
    nxt = next_ref[...]
    ext = jnp.concatenate([jnp.where(is_start, jnp.zeros_like(prev), prev), x_ref[...],
                           jnp.where(is_end, jnp.zeros_like(nxt), nxt)], axis=0)
    w = w_ref[...]
    bias = b_ref[...]
    select = sh_ref[...]
    taps = [k for k in range(width) if k != pad]
    for rb in range(rows // blk):
        shifted = _dot(select, ext[rb * blk:rb * blk + blk + 2 * halo, :])
        acc = bias + w[pad:pad + 1, :] * x_ref[rb * blk:(rb + 1) * blk, :].astype(F32)
        for n, k in enumerate(taps):
            acc = acc + w[k:k + 1, :] * shifted[n * blk:(n + 1) * blk, :]
        o_ref[rb * blk:(rb + 1) * blk, :] = _silu(acc).astype(o_ref.dtype)


def _conv_silu(proj, col_off, conv_w, conv_b, seq_lens):
    t = proj.shape[0]
    width, ch = conv_w.shape
    rows = min(CONV_ROWS, min(seq_lens))
    cols = min(CONV_COLS, ch)
    halo = BF16_SUBLANES
    blk = CONV_BLOCK
    pad = width // 2
    assert col_off % cols == 0 and ch % cols == 0 and all(s % rows == 0 for s in seq_lens)
    assert rows % blk == 0 and pad <= halo
    starts, ends, pos = [], [], 0
    for s in seq_lens:
        starts.append(pos // rows)
        pos += s
        ends.append(pos // rows - 1)
    rb = rows // halo
    nhalo = t // halo
    cb = col_off // cols
    taps = jnp.array([k for k in range(width) if k != pad])
    src = jnp.arange(blk)[None, :] + halo + (taps - pad)[:, None]
    select = (src.reshape(-1)[:, None] == jnp.arange(blk + 2 * halo)[None, :]).astype(BF16)
    return pl.pallas_call(
        functools.partial(_conv_kernel, rows=rows, width=width, start_tiles=tuple(starts), end_tiles=tuple(ends)),
        name="conv_silu", grid=(t // rows, ch // cols),
        in_specs=[pl.BlockSpec((halo, cols), lambda i, j: (jnp.maximum(i * rb - 1, 0), cb + j)),
                  pl.BlockSpec((rows, cols), lambda i, j: (i, cb + j)),
                  pl.BlockSpec((halo, cols), lambda i, j: (jnp.minimum((i + 1) * rb, nhalo - 1), cb + j)),
                  pl.BlockSpec((width, cols), lambda i, j: (0, j)),
                  pl.BlockSpec((1, cols), lambda i, j: (0, j)),
                  pl.BlockSpec(select.shape, lambda i, j: (0, 0))],
        out_specs=pl.BlockSpec((rows, cols), lambda i, j: (i, j)),
        out_shape=jax.ShapeDtypeStruct((t, ch), BF16),
        compiler_params=_cparams(("parallel", "arbitrary")))(
            proj, proj, proj, conv_w, conv_b.reshape(1, ch), select)


def _ssd_kernel(*refs, reverse, n_chunks, reset_chunks, groups, heads, d_inner):
    if reverse:
        (xbc_ref, dt_ref, dtb_ref, alog_ref, tri_ref, trit_ref, exp_ref,
         yf_ref, z_ref, nw_ref, o_ref, state_ref) = refs
    else:
        (xbc_ref, dt_ref, dtb_ref, alog_ref, tri_ref, trit_ref, exp_ref,
         dskip_ref, o_ref, state_ref) = refs
    step = pl.program_id(0)
    chunk = (n_chunks - 1 - step) if reverse else step
    L = SSD_CHUNK
    gw = d_inner // groups
    hpg = heads // groups
    n = SSM_STATE
    col0 = heads if reverse else 0

    @pl.when(functools.reduce(jnp.logical_or, [chunk == c for c in reset_chunks]))
    def _():
        state_ref[...] = jnp.zeros_like(state_ref)

    dt = _softplus(dt_ref[...] + dtb_ref[...])
    dta = dt * (-jnp.exp(alog_ref[...]))
    dt_t = dt.T
    dta_t = dta.T
    tri = tri_ref[...]
    tri_t = trit_ref[...]
    acum = sum(_dot(tri, p) for p in _split_bf16(dta, 3))
    acum_t = sum(_dot(p, tri_t) for p in _split_bf16(dta_t, 3))
    edge = 0 if reverse else L - 1
    a_end = acum[edge:edge + 1, :]
    to_end = jnp.exp(a_end - acum) * dt
    ea = jnp.exp(acum)
    expand = exp_ref[...]
    to_end_x = sum(_dot(p, expand) for p in _split_bf16(to_end, 2))
    ea_x = sum(_dot(p, expand) for p in _split_bf16(ea, 2))
    decay_x = ea_x[edge:edge + 1, :]
    acum2 = acum * LOG2E
    row_t = acum_t * LOG2E - jnp.log2(dt_t)

    row = lax.broadcasted_iota(jnp.int32, (L, L), 0)
    colm = lax.broadcasted_iota(jnp.int32, (L, L), 1)
    causal = (row <= colm) if reverse else (row >= colm)
    lane = lax.broadcasted_iota(jnp.int32, (L, LANES), 1)
    first_half = lane < SSM_HEAD_DIM

    for g in range(groups):
        xs = xbc_ref[:, g * gw:(g + 1) * gw]
        bm = xbc_ref[:, d_inner + g * n:d_inner + (g + 1) * n]
        cm = xbc_ref[:, d_inner + groups * n + g * n:d_inner + groups * n + (g + 1) * n]
        cb = _dot_nt(cm, bm)
        st = state_ref[g]
        y_off = _dot(cm, st.astype(BF16)) * ea_x[:, g * gw:(g + 1) * gw]
        xs_scaled = (xs.astype(F32) * to_end_x[:, g * gw:(g + 1) * gw]).astype(BF16)
        state_ref[g] = st * decay_x[:, g * gw:(g + 1) * gw] + _dot_tn(bm, xs_scaled)
        pairs = []
        for k in range(hpg // 2):
            xp = xs[:, k * LANES:(k + 1) * LANES]
            y_pair = None
            for side in range(2):
                c = col0 + g * hpg + 2 * k + side
                seg = acum2[:, c:c + 1] - row_t[c:c + 1, :]
                m = (cb * jnp.exp2(jnp.where(causal, seg, -jnp.inf))).astype(BF16)
                keep = first_half if side == 0 else jnp.logical_not(first_half)
                part = _dot(m, jnp.where(keep, xp, jnp.zeros_like(xp)))
                y_pair = part if y_pair is None else y_pair + part
            lo = g * gw + k * LANES
            y_pair = y_pair + y_off[:, k * LANES:(k + 1) * LANES]
            if not reverse:
                o_ref[:, lo:lo + LANES] = y_pair + dskip_ref[:, lo:lo + LANES] * xp.astype(F32)
            else:
                pairs.append(y_pair)
        if reverse:
            y = jnp.concatenate(pairs, axis=1) + yf_ref[:, g * gw:(g + 1) * gw]
            y = y * _silu(z_ref[:, g * gw:(g + 1) * gw].astype(F32))
            o_ref[:, g * gw:(g + 1) * gw] = (_rms(y, NORM_EPS) * nw_ref[:, g * gw:(g + 1) * gw]).astype(o_ref.dtype)


def _ssd(xbc, proj, dt_raw, dt_bias, a_log, d_skip, norm_w, seq_lens, groups):
    t = xbc.shape[0]
    heads = d_skip.shape[0]
    d_inner = heads * SSM_HEAD_DIM
    L = SSD_CHUNK
    nc = t // L
    assert 2 * heads <= LANES and (heads // groups) % 2 == 0
    starts, lasts, pos = [], [], 0
    for s in seq_lens:
        assert s % L == 0
        starts.append(pos // L)
        pos += s
        lasts.append(pos // L - 1)
    padh = LANES - 2 * heads
    dtb = jnp.pad(dt_bias.reshape(-1), (0, padh))
    alog = jnp.pad(a_log.reshape(-1), (0, padh))
    tri_lo = jnp.tril(jnp.ones((L, L), F32)).astype(BF16)
    tri_up = jnp.triu(jnp.ones((L, L), F32)).astype(BF16)
    head_of_col = jnp.arange(d_inner) // SSM_HEAD_DIM
    dskip_x = jnp.repeat(d_skip.astype(F32), SSM_HEAD_DIM).reshape(1, d_inner)

    def run(reverse, extra_args, extra_specs, out_dtype):
        cidx = (lambda i: nc - 1 - i) if reverse else (lambda i: i)
        tri = tri_up if reverse else tri_lo
        expand = (jnp.arange(LANES)[:, None] == (head_of_col + (heads if reverse else 0))[None, :]).astype(BF16)
        const = lambda shape: pl.BlockSpec(shape, lambda i: (0, 0))
        in_specs = [pl.BlockSpec((L, xbc.shape[1]), lambda i: (cidx(i), 0)),
                    pl.BlockSpec((L, LANES), lambda i: (cidx(i), 0)),
                    const((1, LANES)), const((1, LANES)),
                    const((L, L)), const((L, L)), const((LANES, d_inner))] + extra_specs(cidx)
        return pl.pallas_call(
            functools.partial(_ssd_kernel, reverse=reverse, n_chunks=nc,
                              reset_chunks=tuple(lasts if reverse else starts),
                              groups=groups, heads=heads, d_inner=d_inner),
            name="ssd_bwd" if reverse else "ssd_fwd", grid=(nc,), in_specs=in_specs,
            out_specs=pl.BlockSpec((L, d_inner), lambda i: (cidx(i), 0)),
            out_shape=jax.ShapeDtypeStruct((t, d_inner), out_dtype),
            scratch_shapes=[pltpu.VMEM((groups, SSM_STATE, d_inner // groups), F32)],
            compiler_params=_cparams(("arbitrary",)))(
                xbc, dt_raw, dtb.reshape(1, LANES), alog.reshape(1, LANES), tri, tri.T, expand, *extra_args)

    y_fwd = run(False, [dskip_x], lambda cidx: [pl.BlockSpec((1, d_inner), lambda i: (0, 0))], F32)
    return run(True, [y_fwd, proj, norm_w.reshape(1, d_inner)],
               lambda cidx: [pl.BlockSpec((L, d_inner), lambda i: (cidx(i), 0)),
                             pl.BlockSpec((L, d_inner), lambda i: (cidx(i), 0)),
                             pl.BlockSpec((1, d_inner), lambda i: (0, 0))], BF16)


def _t5_bucket(rel):
    half = NUM_BUCKETS // 2
    max_exact = half // 2
    n = jnp.abs(rel)
    log_ratio = jnp.log(jnp.maximum(n, 1).astype(F32) / max_exact) / math.log(MAX_DISTANCE / max_exact)
    large = jnp.minimum(max_exact + (log_ratio * (half - max_exact)).astype(jnp.int32), half - 1)
    return jnp.where(rel > 0, half, 0) + jnp.where(n < max_exact, n, large)


def _attn_kernel(far_ref, lq_ref, sw_ref, q_ref, k_ref, v_ref, bias_ref, o_ref, m_ref, l_ref, acc_ref,
                 sa_ref, sb_ref, *, t, nk, lambda_init):
    h = pl.program_id(1)
    i = pl.program_id(2)
    dh = DIFF_HEAD_DIM
    nblk = t // LANES
    c_before = far_ref[h, 0]
    c_after = far_ref[h, 1]
    m_ref[...] = jnp.full(m_ref.shape, -jnp.inf, F32)
    l_ref[...] = jnp.zeros(l_ref.shape, F32)
    acc_ref[...] = jnp.zeros(acc_ref.shape, F32)

    def qk(kb, j, width=1):
        ks = k_ref[pl.ds(pl.multiple_of(kb * t, t), width * t), j * dh:(j + 1) * dh]
        return _dot_nt(q_ref[:, j * dh:(j + 1) * dh], ks)

    def softmax_pv(kb, s_of, shift, width=1):
        vs = v_ref[pl.ds(pl.multiple_of(kb * t, t), width * t), :]
        for j in range(2):
            s = s_of(j)
            blocks = [s[:, c * LANES:(c + 1) * LANES] for c in range(width * nblk)]
            row_max = jnp.max(functools.reduce(jnp.maximum, blocks), axis=-1, keepdims=True)
            m_old = m_ref[j]
            m_new = jnp.maximum(m_old, row_max + shift)
            alpha = jnp.exp2(m_old - m_new)
            ref = m_new - shift
            ps = [jnp.exp2(b - ref) for b in blocks]
            l_ref[j] = alpha * l_ref[j] + functools.reduce(jnp.add, ps)
            p = jnp.concatenate([x.astype(BF16) for x in ps], axis=1)
            acc_ref[j] = jnp.concatenate([alpha] * (acc_ref.shape[2] // LANES), axis=1) * acc_ref[j] + _dot(p, vs)
            m_ref[j] = m_new

    def update(kb, bias_tile):
        softmax_pv(kb, lambda j: qk(kb, j) + bias_tile, 0.0)

    def far_loop(lo, hi, shift):
        w = FAR_W
        n = hi - lo
        nd = n // w
        base = lo + n % w

        def single(kb, carry):
            softmax_pv(kb, lambda j: qk(kb, j), shift)
            return carry

        lax.fori_loop(lo, base, single, 0)

        @pl.when(nd > 0)
        def _():
            for j in range(2):
                sa_ref[j] = qk(base, j, w)

            def pair(kp, carry):
                d0 = base + 2 * w * kp
                d1 = jnp.minimum(d0 + w, hi - w)
                d2 = jnp.minimum(d0 + 2 * w, hi - w)
                for j in range(2):
                    sb_ref[j] = qk(d1, j, w)
                softmax_pv(d0, lambda j: sa_ref[j], shift, w)
                for j in range(2):
                    sa_ref[j] = qk(d2, j, w)
                softmax_pv(d1, lambda j: sb_ref[j], shift, w)
                return carry

            lax.fori_loop(0, nd // 2, pair, 0)

            @pl.when(nd % 2 == 1)
            def _():
                softmax_pv(hi - w, lambda j: sa_ref[j], shift, w)

    far_loop(0, jnp.maximum(i - 1, 0), c_before)

    @pl.when(i >= 1)
    def _():
        update(i - 1, bias_ref[0, 0])

    update(i, bias_ref[0, 1])

    @pl.when(i + 1 < nk)
    def _():
        update(i + 1, bias_ref[0, 2])

    far_loop(jnp.minimum(i + 2, nk), nk, c_after)

    lq = lq_ref[...]
    lam = (jnp.exp(jnp.sum(lq[0:1] * lq[1:2], axis=-1, keepdims=True))
           - jnp.exp(jnp.sum(lq[2:3] * lq[3:4], axis=-1, keepdims=True)) + lambda_init)
    l0 = jnp.sum(l_ref[0], axis=-1, keepdims=True)
    l1 = jnp.sum(l_ref[1], axis=-1, keepdims=True)
    o = acc_ref[0] / l0 - lam * (acc_ref[1] / l1)
    o_ref[...] = (_rms(o, SUBLN_EPS) * sw_ref[...] * (1.0 - lambda_init)).astype(o_ref.dtype)


def _near_bias_tiles(table_t, t):
    x = jnp.concatenate([jnp.arange(t), jnp.arange(t) - t])
    rel = (jnp.arange(-1, 2) * t)[:, None] + x[None, :]
    w = table_t[:, _t5_bucket(rel)]
    flat = jnp.tile(w, (1, 1, t))[..., :t * (2 * t - 1)]
    return flat.reshape(w.shape[0], 3, t, 2 * t - 1)[..., :t]


def _diff_attention(proj, q_off, n_heads, rel_bias, lambda_qk, subln_w, lambda_init, tok_off, bsz, seq):
    t = min(ATTN_TILE, seq)
    nk = seq // t
    hw = 2 * DIFF_HEAD_DIM
    assert t >= MAX_DISTANCE and seq % t == 0 and tok_off % seq == 0 and q_off % hw == 0
    qb = q_off // hw
    table_t = rel_bias.astype(F32).T * LOG2E
    bias_near = _near_bias_tiles(table_t, t)
    far = table_t[:, _t5_bucket(jnp.array([-2 * MAX_DISTANCE, 2 * MAX_DISTANCE]))]
    row0 = tok_off // t
    seq0 = tok_off // seq
    return pl.pallas_call(
        functools.partial(_attn_kernel, t=t, nk=nk, lambda_init=lambda_init),
        name="diff_attn", grid=(bsz, n_heads, nk),
        in_specs=[pl.BlockSpec(memory_space=pltpu.SMEM),
                  pl.BlockSpec((4, DIFF_HEAD_DIM), lambda b, h, i: (0, 0)),
                  pl.BlockSpec((1, hw), lambda b, h, i: (0, 0)),
                  pl.BlockSpec((t, hw), lambda b, h, i: (row0 + b * nk + i, qb + h)),
                  pl.BlockSpec((seq, hw), lambda b, h, i: (seq0 + b, qb + n_heads + h),
                               pipeline_mode=pl.Buffered(1)),
                  pl.BlockSpec((seq, hw), lambda b, h, i: (seq0 + b, qb + 2 * n_heads + h),
                               pipeline_mode=pl.Buffered(1)),
                  pl.BlockSpec((1, 3, t, t), lambda b, h, i: (h, 0, 0, 0))],
        out_specs=pl.BlockSpec((t, hw), lambda b, h, i: (b * nk + i, h)),
        out_shape=jax.ShapeDtypeStruct((bsz * seq, n_heads * hw), BF16),
        scratch_shapes=[pltpu.VMEM((2, t, LANES), F32), pltpu.VMEM((2, t, LANES), F32),
                        pltpu.VMEM((2, t, hw), F32), pltpu.VMEM((2, t, FAR_W * t), F32),
                        pltpu.VMEM((2, t, FAR_W * t), F32)],
        compiler_params=_cparams(("parallel", "parallel", "arbitrary")))(
            far, lambda_qk.astype(F32), subln_w.reshape(1, hw).astype(F32), proj, proj, proj, bias_near)


def _gelu_tanh(x):
    return 0.5 * x * (1.0 + jnp.tanh(math.sqrt(2.0 / math.pi) * (x + 0.044715 * (x * x * x))))


def _gmlp_kernel(u_ref, v_ref, lng_ref, lnb_ref, ws_ref, bs_ref, o_ref, *, rows, groups):
    u = _gelu_tanh(u_ref[...].astype(F32))
    v = _gelu_tanh(v_ref[...].astype(F32))
    mu = jnp.mean(v, axis=-1, keepdims=True)
    vc = v - mu
    vn = (vc * lax.rsqrt(jnp.mean(vc * vc, axis=-1, keepdims=True) + LN_EPS) * lng_ref[...] + lnb_ref[...]).astype(BF16)
    gd = v.shape[1] // groups
    L = GMLP_CHUNK
    for c in range(rows // L):
        for g in range(groups):
            mixed = _dot(ws_ref[g], vn[c * L:(c + 1) * L, g * gd:(g + 1) * gd]) + bs_ref[:, g * gd:(g + 1) * gd]
            o_ref[c * L:(c + 1) * L, g * gd:(g + 1) * gd] = (
                u[c * L:(c + 1) * L, g * gd:(g + 1) * gd] * mixed).astype(o_ref.dtype)


def _gmlp(proj, u_off, width, ln_v, w_s, b_s):
    t = proj.shape[0]
    groups = w_s.shape[0]
    L = GMLP_CHUNK
    rows = min(GMLP_ROWS, t)
    assert u_off % width == 0 and rows % L == 0
    ub = u_off // width
    bias_x = jnp.repeat(b_s.astype(F32).T, width // groups, axis=1)
    vec = pl.BlockSpec((1, width), lambda i: (0, 0))
    return pl.pallas_call(
        functools.partial(_gmlp_kernel, rows=rows, groups=groups), name="gmlp", grid=(t // rows,),
        in_specs=[pl.BlockSpec((rows, width), lambda i: (i, ub)),
                  pl.BlockSpec((rows, width), lambda i: (i, ub + 1)),
                  vec, vec,
                  pl.BlockSpec((groups, L, L), lambda i: (0, 0, 0)),
                  pl.BlockSpec((L, width), lambda i: (0, 0))],
        out_specs=pl.BlockSpec((rows, width), lambda i: (i, 0)),
        out_shape=jax.ShapeDtypeStruct((t, width), BF16),
        compiler_params=_cparams(("parallel",)))(
            proj, proj, ln_v[0].reshape(1, width).astype(F32), ln_v[1].reshape(1, width).astype(F32),
            w_s.astype(BF16), bias_x)


def _merge_kernel(h_ref, s_ref, a0_ref, a1_ref, m_ref, wg0_ref, wg1_ref, wg2_ref, wos_ref, woa_ref, wom_ref,
                  o_ref, *, first_blocks):
    def gated_sum(a_ref):
        h = h_ref[...]
        merged = (jax.nn.sigmoid(_dot(h, wg0_ref[...])) * _dot(s_ref[...], wos_ref[...])
                  + jax.nn.sigmoid(_dot(h, wg1_ref[...])) * _dot(a_ref[...], woa_ref[...])
                  + jax.nn.sigmoid(_dot(h, wg2_ref[...])) * _dot(m_ref[...], wom_ref[...]))
        o_ref[...] = merged.astype(o_ref.dtype)

    in_first = pl.program_id(0) < first_blocks
    pl.when(in_first)(lambda: gated_sum(a0_ref))
    pl.when(jnp.logical_not(in_first))(lambda: gated_sum(a1_ref))


def _merge(h, y_s, y_a_parts, y_m, w_gate, w_os, w_oa, w_om):
    t, d = h.shape
    ya0, ya1 = y_a_parts
    tm = min(MERGE_TM, ya0.shape[0], ya1.shape[0])
    tn = min(MERGE_TN, d)
    nj = d // tn
    na = ya0.shape[0] // tm
    assert ya0.shape[0] % tm == 0 and ya1.shape[0] % tm == 0
    act = lambda a: pl.BlockSpec((tm, a.shape[1]), lambda i, j: (i, 0))
    wcol = lambda w, off: pl.BlockSpec((w.shape[0], tn), lambda i, j: (0, j + off))
    return pl.pallas_call(
        functools.partial(_merge_kernel, first_blocks=na), name="merge", grid=(t // tm, nj),
        in_specs=[act(h), act(y_s),
                  pl.BlockSpec((tm, ya0.shape[1]), lambda i, j: (jnp.minimum(i, na - 1), 0)),
                  pl.BlockSpec((tm, ya1.shape[1]), lambda i, j: (jnp.maximum(i - na, 0), 0)),
                  act(y_m),
                  wcol(w_gate, 0), wcol(w_gate, nj), wcol(w_gate, 2 * nj),
                  wcol(w_os, 0), wcol(w_oa, 0), wcol(w_om, 0)],
        out_specs=pl.BlockSpec((tm, tn), lambda i, j: (i, j)),
        out_shape=jax.ShapeDtypeStruct((t, d), BF16),
        compiler_params=_cparams(("parallel", "arbitrary")))(
            h, y_s, ya0, ya1, y_m, w_gate, w_gate, w_gate, w_os, w_oa, w_om)


def kernel(x_prompt, x_sample, rel_bias, norms, w_ffn1_in, w_ffn1_out, w_in, conv_w, conv_b, dt_bias, a_log,
           d_skip, ssm_norm, w_o_ssm, lambda_qk, diff_subln, w_o_diff, ln_v, w_spatial, b_spatial, w_o_gmlp,
           w_out, w_ffn2_in, w_ffn2_out):
    depth = norms.shape[0]
    d = x_prompt.shape[-1]
    d_inner = w_o_ssm.shape[1]
    conv_ch = conv_w.shape[2]
    heads = d_skip.shape[1]
    groups = (conv_ch - d_inner) // (2 * SSM_STATE)
    diff_w = w_o_diff.shape[1]
    diff_heads = diff_w // (2 * DIFF_HEAD_DIM)
    gmlp_w = w_o_gmlp.shape[1]
    pb, ps = x_prompt.shape[:2]
    sb, ss = x_sample.shape[:2]
    seq_lens = [ps] * pb + [ss] * sb
    n_prompt = pb * ps

    c_dt = d_inner + conv_ch
    c_diff = c_dt + 2 * heads
    c_gmlp = c_diff + 3 * diff_w
    c_gate = c_gmlp + 2 * gmlp_w
    q_off = c_dt
    u_off = q_off + 3 * diff_w
    n_main = u_off + 2 * gmlp_w
    col_scale = jnp.ones((1, n_main), F32).at[:, q_off:q_off + diff_w].set(DIFF_HEAD_DIM ** -0.5 * LOG2E)

    x = [x_prompt.reshape(n_prompt, d), x_sample.reshape(sb * ss, d)]
    h = _rmsnorm(x, norms[0, 0])
    for l in range(depth):
        n = norms[l]
        lambda_init = 0.8 - 0.6 * math.exp(-0.3 * l)
        wi = w_in[l]
        w_main = jnp.concatenate([wi[:, :c_dt], wi[:, c_diff:c_gate]], axis=1).astype(BF16)
        w_dt = jnp.pad(wi[:, c_dt:c_diff], ((0, 0), (0, LANES - 2 * heads))).astype(BF16)
        w_gate = wi[:, c_gate:].astype(BF16)

        y = _matmul(_matmul_swiglu(h, w_ffn1_in[l].astype(BF16)), w_ffn1_out[l].astype(BF16), BF16, tn=FFN_OUT_TN)
        x, h = _resid_norm(x, y, n[1], 0.5, n[2])
        x = [x]

        proj = _matmul(h, w_main, BF16, scale=col_scale)
        dt_raw = _matmul(h, w_dt, F32, tn=LANES)
        xbc = _conv_silu(proj, d_inner, conv_w[l], conv_b[l], seq_lens)
        y_ssm = _ssd(xbc, proj, dt_raw, dt_bias[l], a_log[l], d_skip[l], ssm_norm[l], seq_lens, groups)
        y_att = [
            _diff_attention(proj, q_off, diff_heads, rel_bias, lambda_qk[l], diff_subln[l], lambda_init, 0, pb, ps),
            _diff_attention(proj, q_off, diff_heads, rel_bias, lambda_qk[l], diff_subln[l], lambda_init,
                            n_prompt, sb, ss)]
        y_gmlp = _gmlp(proj, u_off, gmlp_w, ln_v[l], w_spatial[l], b_spatial[l])
        merged = _merge(h, y_ssm, y_att, y_gmlp, w_gate, w_o_ssm[l].astype(BF16), w_o_diff[l].astype(BF16),
                        w_o_gmlp[l].astype(BF16))
        y = _matmul(merged, w_out[l].astype(BF16), BF16)
        x, h = _resid_norm(x, y, n[3], 1.0, n[4])
        x = [x]

        y = _matmul(_matmul_swiglu(h, w_ffn2_in[l].astype(BF16)), w_ffn2_out[l].astype(BF16), BF16, tn=FFN_OUT_TN)
        if l + 1 < depth:
            x, h = _resid_norm(x, y, n[5], 0.5, norms[l + 1, 0])
            x = [x]

    y_prompt = _resid_rows(x[0], y, norms[depth - 1, 5], 0.5, 0, n_prompt)
    y_sample = _resid_rows(x[0], y, norms[depth - 1, 5], 0.5, n_prompt, sb * ss)
    return y_prompt.reshape(x_prompt.shape), y_sample.reshape(x_sample.shape)
```

```python
import functools
import math

import jax
import jax.numpy as jnp
from jax import lax
from jax.experimental import pallas as pl
from jax.experimental.pallas import tpu as pltpu

F32 = jnp.float32
BF16 = jnp.bfloat16

SSM_HEAD_DIM = 64
SSM_STATE = 128
SSD_CHUNK = 128
DIFF_HEAD_DIM = 128
NUM_BUCKETS = 32
MAX_DISTANCE = 128
GMLP_CHUNK = 128
NORM_EPS = 1e-6
SUBLN_EPS = 1e-5
LN_EPS = 1e-5
LOG2E = math.log2(math.e)

LANES = 128
BF16_SUBLANES = 16
VMEM_LIMIT_BYTES = 56 * 1024 * 1024

MM_TM = 1024
MM_TN = 1024
FFN_OUT_TN = 2048
ROW_TILE = 256
CONV_ROWS = 1024
CONV_COLS = 1024
CONV_BLOCK = 128
ATTN_TILE = 512
FAR_W = 4
MERGE_TM = 512
MERGE_TN = 256
GMLP_ROWS = 512
SSD_STEP = 4


def _cparams(sem):
    return pltpu.CompilerParams(dimension_semantics=sem, vmem_limit_bytes=VMEM_LIMIT_BYTES)


def _dot(a, b):
    return jnp.dot(a, b, preferred_element_type=F32)


def _dot_nt(a, b):
    return lax.dot_general(a, b, (((1,), (1,)), ((), ())), preferred_element_type=F32)


def _dot_tn(a, b):
    return lax.dot_general(a, b, (((0,), (0,)), ((), ())), preferred_element_type=F32)


def _split_bf16(x, n):
    parts = []
    r = x
    for _ in range(n):
        p = r.astype(BF16)
        parts.append(p)
        r = r - p.astype(F32)
    return parts


def _rms(x, eps):
    return x * lax.rsqrt(jnp.mean(x * x, axis=-1, keepdims=True) + eps)


def _silu(x):
    return x * jax.nn.sigmoid(x)


def _softplus(x):
    return jnp.maximum(x, 0.0) + jnp.log1p(jnp.exp(-jnp.abs(x)))


def _mm_scale_kernel(a_ref, w_ref, s_ref, o_ref):
    o_ref[...] = (_dot(a_ref[...], w_ref[...]) * s_ref[...]).astype(o_ref.dtype)


def _mm_kernel(a_ref, w_ref, o_ref):
    o_ref[...] = _dot(a_ref[...], w_ref[...]).astype(o_ref.dtype)


def _mm_swiglu_kernel(a_ref, wg_ref, wu_ref, o_ref):
    a = a_ref[...]
    g = _dot(a, wg_ref[...])
    u = _dot(a, wu_ref[...])
    o_ref[...] = (_silu(g) * u).astype(o_ref.dtype)


def _matmul(a, w, out_dtype, scale=None, tm=None, tn=None):
    m, k = a.shape
    n = w.shape[1]
    tm = min(tm or MM_TM, m)
    tn = min(tn or MM_TN, n)
    in_specs = [pl.BlockSpec((tm, k), lambda i, j: (i, 0)), pl.BlockSpec((k, tn), lambda i, j: (0, j))]
    args = [a, w]
    body = _mm_kernel
    if scale is not None:
        in_specs.append(pl.BlockSpec((1, tn), lambda i, j: (0, j)))
        args.append(scale)
        body = _mm_scale_kernel
    return pl.pallas_call(
        body, name="mm", grid=(m // tm, n // tn), in_specs=in_specs,
        out_specs=pl.BlockSpec((tm, tn), lambda i, j: (i, j)),
        out_shape=jax.ShapeDtypeStruct((m, n), out_dtype),
        compiler_params=_cparams(("parallel", "arbitrary")))(*args)


def _matmul_swiglu(a, w):
    m, k = a.shape
    n = w.shape[1] // 2
    tm = min(MM_TM, m)
    tn = min(MM_TN // 2, n)
    nj = n // tn
    return pl.pallas_call(
        _mm_swiglu_kernel, name="mm_swiglu", grid=(m // tm, nj),
        in_specs=[pl.BlockSpec((tm, k), lambda i, j: (i, 0)),
                  pl.BlockSpec((k, tn), lambda i, j: (0, j)),
                  pl.BlockSpec((k, tn), lambda i, j: (0, j + nj))],
        out_specs=pl.BlockSpec((tm, tn), lambda i, j: (i, j)),
        out_shape=jax.ShapeDtypeStruct((m, n), BF16),
        compiler_params=_cparams(("parallel", "arbitrary")))(a, w, w)


def _stream_specs(parts, tr):
    d = parts[0].shape[1]
    if len(parts) == 1:
        return [pl.BlockSpec((tr, d), lambda i: (i, 0))], None
    na = parts[0].shape[0] // tr
    return [pl.BlockSpec((tr, d), lambda i: (jnp.minimum(i, na - 1), 0)),
            pl.BlockSpec((tr, d), lambda i: (jnp.maximum(i - na, 0), 0))], na


def _stream_block(x_refs, first_blocks):
    if len(x_refs) == 1:
        return x_refs[0][...]
    return jnp.where(pl.program_id(0) < first_blocks, x_refs[0][...], x_refs[1][...])


def _rmsnorm_kernel(*refs, n_parts, first_blocks):
    g_ref, o_ref = refs[n_parts:]
    o_ref[...] = (_rms(_stream_block(refs[:n_parts], first_blocks), NORM_EPS) * g_ref[...]).astype(o_ref.dtype)


def _rmsnorm(parts, g):
    d = parts[0].shape[1]
    t = sum(p.shape[0] for p in parts)
    tr = min(ROW_TILE, min(p.shape[0] for p in parts))
    x_specs, first_blocks = _stream_specs(parts, tr)
    return pl.pallas_call(
        functools.partial(_rmsnorm_kernel, n_parts=len(parts), first_blocks=first_blocks),
        name="rmsnorm", grid=(t // tr,),
        in_specs=x_specs + [pl.BlockSpec((1, d), lambda i: (0, 0))],
        out_specs=pl.BlockSpec((tr, d), lambda i: (i, 0)),
        out_shape=jax.ShapeDtypeStruct((t, d), BF16),
        compiler_params=_cparams(("parallel",)))(*parts, g.reshape(1, d))


def _resid_norm_kernel(*refs, n_parts, first_blocks, scale):
    y_ref, g1_ref, g2_ref, xo_ref, h_ref = refs[n_parts:]
    xn = _stream_block(refs[:n_parts], first_blocks) + scale * (_rms(y_ref[...].astype(F32), NORM_EPS) * g1_ref[...])
    xo_ref[...] = xn
    h_ref[...] = (_rms(xn, NORM_EPS) * g2_ref[...]).astype(h_ref.dtype)


def _resid_kernel(x_ref, y_ref, g1_ref, xo_ref, *, scale):
    xo_ref[...] = x_ref[...] + scale * (_rms(y_ref[...].astype(F32), NORM_EPS) * g1_ref[...])


def _resid_norm(parts, y, g_post, scale, g_next):
    t, d = y.shape
    tr = min(ROW_TILE, min(p.shape[0] for p in parts))
    x_specs, first_blocks = _stream_specs(parts, tr)
    row = pl.BlockSpec((tr, d), lambda i: (i, 0))
    vec = pl.BlockSpec((1, d), lambda i: (0, 0))
    return pl.pallas_call(
        functools.partial(_resid_norm_kernel, n_parts=len(parts), first_blocks=first_blocks, scale=scale),
        name="resid_norm", grid=(t // tr,),
        in_specs=x_specs + [row, vec, vec], out_specs=[row, row],
        out_shape=[jax.ShapeDtypeStruct((t, d), F32), jax.ShapeDtypeStruct((t, d), BF16)],
        compiler_params=_cparams(("parallel",)))(*parts, y, g_post.reshape(1, d), g_next.reshape(1, d))


def _resid_rows(x, y, g_post, scale, row_off, n_rows):
    d = x.shape[1]
    tr = min(ROW_TILE, n_rows)
    assert row_off % tr == 0 and n_rows % tr == 0
    b0 = row_off // tr
    row = pl.BlockSpec((tr, d), lambda i: (b0 + i, 0))
    return pl.pallas_call(
        functools.partial(_resid_kernel, scale=scale), name="resid", grid=(n_rows // tr,),
        in_specs=[row, row, pl.BlockSpec((1, d), lambda i: (0, 0))],
        out_specs=pl.BlockSpec((tr, d), lambda i: (i, 0)),
        out_shape=jax.ShapeDtypeStruct((n_rows, d), F32),
        compiler_params=_cparams(("parallel",)))(x, y, g_post.reshape(1, d))


def _conv_kernel(prev_ref, x_ref, next_ref, w_ref, b_ref, sh_ref, o_ref, *, rows, width, start_tiles, end_tiles):
    i = pl.program_id(0)
    halo = BF16_SUBLANES
    pad = width // 2
    blk = CONV_BLOCK
    is_start = functools.reduce(jnp.logical_or, [i == s for s in start_tiles])
    is_end = functools.reduce(jnp.logical_or, [i == e for e in end_tiles])
    prev = prev_ref[...]
    nxt = next_ref[...]
    ext = jnp.concatenate([jnp.where(is_start, jnp.zeros_like(prev), prev), x_ref[...],
                           jnp.where(is_end, jnp.zeros_like(nxt), nxt)], axis=0)
    w = w_ref[...]
    bias = b_ref[...]
    select = sh_ref[...]
    taps = [k for k in range(width) if k != pad]
    for rb in range(rows // blk):
        shifted = _dot(select, ext[rb * blk:rb * blk + blk + 2 * halo, :])
        acc = bias + w[pad:pad + 1, :] * x_ref[rb * blk:(rb + 1) * blk, :].astype(F32)
        for n, k in enumerate(taps):
            acc = acc + w[k:k + 1, :] * shifted[n * blk:(n + 1) * blk, :]
        o_ref[rb * blk:(rb + 1) * blk, :] = _silu(acc).astype(o_ref.dtype)


def _conv_silu(proj, col_off, conv_w, conv_b, seq_lens):
    t = proj.shape[0]
    width, ch = conv_w.shape
    rows = min(CONV_ROWS, min(seq_lens))
    cols = min(CONV_COLS, ch)
    halo = BF16_SUBLANES
    blk = CONV_BLOCK
    pad = width // 2
    assert col_off % cols == 0 and ch % cols == 0 and all(s % rows == 0 for s in seq_lens)
    assert rows % blk == 0 and pad <= halo
    starts, ends, pos = [], [], 0
    for s in seq_lens:
        starts.append(pos // rows)
        pos += s
        ends.append(pos // rows - 1)
    rb = rows // halo
    nhalo = t // halo
    cb = col_off // cols
    taps = jnp.array([k for k in range(width) if k != pad])
    src = jnp.arange(blk)[None, :] + halo + (taps - pad)[:, None]
    select = (src.reshape(-1)[:, None] == jnp.arange(blk + 2 * halo)[None, :]).astype(BF16)
    return pl.pallas_call(
        functools.partial(_conv_kernel, rows=rows, width=width, start_tiles=tuple(starts), end_tiles=tuple(ends)),
        name="conv_silu", grid=(t // rows, ch // cols),
        in_specs=[pl.BlockSpec((halo, cols), lambda i, j: (jnp.maximum(i * rb - 1, 0), cb + j)),
                  pl.BlockSpec((rows, cols), lambda i, j: (i, cb + j)),
                  pl.BlockSpec((halo, cols), lambda i, j: (jnp.minimum((i + 1) * rb, nhalo - 1), cb + j)),
                  pl.BlockSpec((width, cols), lambda i, j: (0, j)),
                  pl.BlockSpec((1, cols), lambda i, j: (0, j)),
                  pl.BlockSpec(select.shape, lambda i, j: (0, 0))],
        out_specs=pl.BlockSpec((rows, cols), lambda i, j: (i, j)),
        out_shape=jax.ShapeDtypeStruct((t, ch), BF16),
        compiler_params=_cparams(("parallel", "arbitrary")))(
            proj, proj, proj, conv_w, conv_b.reshape(1, ch), select)


def _ssd_kernel(*refs, reverse, n_chunks, reset_chunks, groups, heads, d_inner):
    if reverse:
        (xbc_ref, dt_ref, dtb_ref, alog_ref, tri_ref, trit_ref, exp_ref,
         yf_ref, z_ref, nw_ref, o_ref, state_ref) = refs
    else:
        (xbc_ref, dt_ref, dtb_ref, alog_ref, tri_ref, trit_ref, exp_ref,
         dskip_ref, o_ref, state_ref) = refs
    step = pl.program_id(0)
    L = SSD_CHUNK
    gw = d_inner // groups
    hpg = heads // groups
    n = SSM_STATE
    col0 = heads if reverse else 0
    edge = 0 if reverse else L - 1
    tri = tri_ref[...]
    tri_t = trit_ref[...]
    expand = exp_ref[...]
    row = lax.broadcasted_iota(jnp.int32, (L, L), 0)
    colm = lax.broadcasted_iota(jnp.int32, (L, L), 1)
    causal = (row <= colm) if reverse else (row >= colm)
    lane = lax.broadcasted_iota(jnp.int32, (L, LANES), 1)
    first_half = lane < SSM_HEAD_DIM

    @pl.when(step == 0)
    def _():
        state_ref[...] = jnp.zeros_like(state_ref)

    def one_chunk(sub):
        rows = slice(sub * L, (sub + 1) * L)
        block = (n_chunks // SSD_STEP - 1 - step) if reverse else step
        chunk = block * SSD_STEP + sub
        fresh = functools.reduce(jnp.logical_or, [chunk == c for c in reset_chunks])

        dt = _softplus(dt_ref[rows, :] + dtb_ref[...])
        dta = dt * (-jnp.exp(alog_ref[...]))
        dt_t = dt.T
        acum = sum(_dot(tri, p) for p in _split_bf16(dta, 3))
        acum_t = sum(_dot(p, tri_t) for p in _split_bf16(dta.T, 3))
        to_end = jnp.exp(acum[edge:edge + 1, :] - acum) * dt
        ea = jnp.exp(acum)
        to_end_x = sum(_dot(p, expand) for p in _split_bf16(to_end, 2))
        ea_x = sum(_dot(p, expand) for p in _split_bf16(ea, 2))
        decay_x = ea_x[edge:edge + 1, :]
        acum2 = acum * LOG2E
        row_t = acum_t * LOG2E - jnp.log2(dt_t)

        for g in range(groups):
            cols = slice(g * gw, (g + 1) * gw)
            xs = xbc_ref[rows, cols]
            bm = xbc_ref[rows, d_inner + g * n:d_inner + (g + 1) * n]
            cm = xbc_ref[rows, d_inner + groups * n + g * n:d_inner + groups * n + (g + 1) * n]
            cb = _dot_nt(cm, bm)
            st = jnp.where(fresh, 0.0, state_ref[g])
            y_off = _dot(cm, st.astype(BF16)) * ea_x[:, cols]
            xs_scaled = (xs.astype(F32) * to_end_x[:, cols]).astype(BF16)
            state_ref[g] = st * decay_x[:, cols] + _dot_tn(bm, xs_scaled)
            pairs = []
            for k in range(hpg // 2):
                xp = xs[:, k * LANES:(k + 1) * LANES]
                y_pair = None
                for side in range(2):
                    c = col0 + g * hpg + 2 * k + side
                    seg = acum2[:, c:c + 1] - row_t[c:c + 1, :]
                    m = (cb * jnp.exp2(jnp.where(causal, seg, -jnp.inf))).astype(BF16)
                    keep = first_half if side == 0 else jnp.logical_not(first_half)
                    part = _dot(m, jnp.where(keep, xp, jnp.zeros_like(xp)))
                    y_pair = part if y_pair is None else y_pair + part
                lo = g * gw + k * LANES
                y_pair = y_pair + y_off[:, k * LANES:(k + 1) * LANES]
                if not reverse:
                    o_ref[rows, lo:lo + LANES] = y_pair + dskip_ref[:, lo:lo + LANES] * xp.astype(F32)
                else:
                    pairs.append(y_pair)
            if reverse:
                y = jnp.concatenate(pairs, axis=1) + yf_ref[rows, cols]
                y = y * _silu(z_ref[rows, cols].astype(F32))
                o_ref[rows, cols] = (_rms(y, NORM_EPS) * nw_ref[:, cols]).astype(o_ref.dtype)

    for sub in (reversed(range(SSD_STEP)) if reverse else range(SSD_STEP)):
        one_chunk(sub)


def _ssd(xbc, proj, dt_raw, dt_bias, a_log, d_skip, norm_w, seq_lens, groups):
    t = xbc.shape[0]
    heads = d_skip.shape[0]
    d_inner = heads * SSM_HEAD_DIM
    L = SSD_CHUNK
    nc = t // L
    assert 2 * heads <= LANES and (heads // groups) % 2 == 0
    starts, lasts, pos = [], [], 0
    for s in seq_lens:
        assert s % L == 0
        starts.append(pos // L)
        pos += s
        lasts.append(pos // L - 1)
    padh = LANES - 2 * heads
    dtb = jnp.pad(dt_bias.reshape(-1), (0, padh))
    alog = jnp.pad(a_log.reshape(-1), (0, padh))
    tri_lo = jnp.tril(jnp.ones((L, L), F32)).astype(BF16)
    tri_up = jnp.triu(jnp.ones((L, L), F32)).astype(BF16)
    head_of_col = jnp.arange(d_inner) // SSM_HEAD_DIM
    dskip_x = jnp.repeat(d_skip.astype(F32), SSM_HEAD_DIM).reshape(1, d_inner)

    assert nc % SSD_STEP == 0
    nb = nc // SSD_STEP
    R = SSD_STEP * L

    def run(reverse, extra_args, extra_specs, out_dtype):
        cidx = (lambda i: nb - 1 - i) if reverse else (lambda i: i)
        tri = tri_up if reverse else tri_lo
        expand = (jnp.arange(LANES)[:, None] == (head_of_col + (heads if reverse else 0))[None, :]).astype(BF16)
        const = lambda shape: pl.BlockSpec(shape, lambda i: (0, 0))
        in_specs = [pl.BlockSpec((R, xbc.shape[1]), lambda i: (cidx(i), 0)),
                    pl.BlockSpec((R, LANES), lambda i: (cidx(i), 0)),
                    const((1, LANES)), const((1, LANES)),
                    const((L, L)), const((L, L)), const((LANES, d_inner))] + extra_specs(cidx)
        return pl.pallas_call(
            functools.partial(_ssd_kernel, reverse=reverse, n_chunks=nc,
                              reset_chunks=tuple(lasts if reverse else starts),
                              groups=groups, heads=heads, d_inner=d_inner),
            name="ssd_bwd" if reverse else "ssd_fwd", grid=(nb,), in_specs=in_specs,
            out_specs=pl.BlockSpec((R, d_inner), lambda i: (cidx(i), 0)),
            out_shape=jax.ShapeDtypeStruct((t, d_inner), out_dtype),
            scratch_shapes=[pltpu.VMEM((groups, SSM_STATE, d_inner // groups), F32)],
            compiler_params=_cparams(("arbitrary",)))(
                xbc, dt_raw, dtb.reshape(1, LANES), alog.reshape(1, LANES), tri, tri.T, expand, *extra_args)

    y_fwd = run(False, [dskip_x], lambda cidx: [pl.BlockSpec((1, d_inner), lambda i: (0, 0))], F32)
    return run(True, [y_fwd, proj, norm_w.reshape(1, d_inner)],
               lambda cidx: [pl.BlockSpec((R, d_inner), lambda i: (cidx(i), 0)),
                             pl.BlockSpec((R, d_inner), lambda i: (cidx(i), 0)),
                             pl.BlockSpec((1, d_inner), lambda i: (0, 0))], BF16)


def _t5_bucket(rel):
    half = NUM_BUCKETS // 2
    max_exact = half // 2
    n = jnp.abs(rel)
    log_ratio = jnp.log(jnp.maximum(n, 1).astype(F32) / max_exact) / math.log(MAX_DISTANCE / max_exact)
    large = jnp.minimum(max_exact + (log_ratio * (half - max_exact)).astype(jnp.int32), half - 1)
    return jnp.where(rel > 0, half, 0) + jnp.where(n < max_exact, n, large)


def _attn_kernel(far_ref, lq_ref, sw_ref, q_ref, k_ref, v_ref, bias_ref, o_ref, m_ref, l_ref, acc_ref,
                 sa_ref, sb_ref, *, t, nk, lambda_init):
    h = pl.program_id(1)
    i = pl.program_id(2)
    dh = DIFF_HEAD_DIM
    nblk = t // LANES
    c_before = far_ref[h, 0]
    c_after = far_ref[h, 1]
    m_ref[...] = jnp.full(m_ref.shape, -jnp.inf, F32)
    l_ref[...] = jnp.zeros(l_ref.shape, F32)
    acc_ref[...] = jnp.zeros(acc_ref.shape, F32)

    def qk(kb, j, width=1):
        ks = k_ref[pl.ds(pl.multiple_of(kb * t, t), width * t), j * dh:(j + 1) * dh]
        return _dot_nt(q_ref[:, j * dh:(j + 1) * dh], ks)

    def softmax_pv(kb, s_of, shift, width=1):
        vs = v_ref[pl.ds(pl.multiple_of(kb * t, t), width * t), :]
        for j in range(2):
            s = s_of(j)
            blocks = [s[:, c * LANES:(c + 1) * LANES] for c in range(width * nblk)]
            row_max = jnp.max(functools.reduce(jnp.maximum, blocks), axis=-1, keepdims=True)
            m_old = m_ref[j]
            m_new = jnp.maximum(m_old, row_max + shift)
            alpha = jnp.exp2(m_old - m_new)
            ref = m_new - shift
            ps = [jnp.exp2(b - ref) for b in blocks]
            l_ref[j] = alpha * l_ref[j] + functools.reduce(jnp.add, ps)
            p = jnp.concatenate([x.astype(BF16) for x in ps], axis=1)
            acc_ref[j] = jnp.concatenate([alpha] * (acc_ref.shape[2] // LANES), axis=1) * acc_ref[j] + _dot(p, vs)
            m_ref[j] = m_new

    def update(kb, bias_tile):
        softmax_pv(kb, lambda j: qk(kb, j) + bias_tile, 0.0)

    def far_loop(lo, hi, shift):
        w = FAR_W
        n = hi - lo
        nd = n // w
        base = lo + n % w

        def single(kb, carry):
            softmax_pv(kb, lambda j: qk(kb, j), shift)
            return carry

        lax.fori_loop(lo, base, single, 0)

        @pl.when(nd > 0)
        def _():
            for j in range(2):
                sa_ref[j] = qk(base, j, w)

            def pair(kp, carry):
                d0 = base + 2 * w * kp
                d1 = jnp.minimum(d0 + w, hi - w)
                d2 = jnp.minimum(d0 + 2 * w, hi - w)
                for j in range(2):
                    sb_ref[j] = qk(d1, j, w)
                softmax_pv(d0, lambda j: sa_ref[j], shift, w)
                for j in range(2):
                    sa_ref[j] = qk(d2, j, w)
                softmax_pv(d1, lambda j: sb_ref[j], shift, w)
                return carry

            lax.fori_loop(0, nd // 2, pair, 0)

            @pl.when(nd % 2 == 1)
            def _():
                softmax_pv(hi - w, lambda j: sa_ref[j], shift, w)

    far_loop(0, jnp.maximum(i - 1, 0), c_before)

    @pl.when(i >= 1)
    def _():
        update(i - 1, bias_ref[0, 0])

    update(i, bias_ref[0, 1])

    @pl.when(i + 1 < nk)
    def _():
        update(i + 1, bias_ref[0, 2])

    far_loop(jnp.minimum(i + 2, nk), nk, c_after)

    lq = lq_ref[...]
    lam = (jnp.exp(jnp.sum(lq[0:1] * lq[1:2], axis=-1, keepdims=True))
           - jnp.exp(jnp.sum(lq[2:3] * lq[3:4], axis=-1, keepdims=True)) + lambda_init)
    l0 = jnp.sum(l_ref[0], axis=-1, keepdims=True)
    l1 = jnp.sum(l_ref[1], axis=-1, keepdims=True)
    o = acc_ref[0] / l0 - lam * (acc_ref[1] / l1)
    o_ref[...] = (_rms(o, SUBLN_EPS) * sw_ref[...] * (1.0 - lambda_init)).astype(o_ref.dtype)


def _near_bias_tiles(table_t, t):
    x = jnp.concatenate([jnp.arange(t), jnp.arange(t) - t])
    rel = (jnp.arange(-1, 2) * t)[:, None] + x[None, :]
    w = table_t[:, _t5_bucket(rel)]
    flat = jnp.tile(w, (1, 1, t))[..., :t * (2 * t - 1)]
    return flat.reshape(w.shape[0], 3, t, 2 * t - 1)[..., :t]


def _diff_attention(proj, q_off, n_heads, rel_bias, lambda_qk, subln_w, lambda_init, tok_off, bsz, seq):
    t = min(ATTN_TILE, seq)
    nk = seq // t
    hw = 2 * DIFF_HEAD_DIM
    assert t >= MAX_DISTANCE and seq % t == 0 and tok_off % seq == 0 and q_off % hw == 0
    qb = q_off // hw
    table_t = rel_bias.astype(F32).T * LOG2E
    bias_near = _near_bias_tiles(table_t, t)
    far = table_t[:, _t5_bucket(jnp.array([-2 * MAX_DISTANCE, 2 * MAX_DISTANCE]))]
    row0 = tok_off // t
    seq0 = tok_off // seq
    return pl.pallas_call(
        functools.partial(_attn_kernel, t=t, nk=nk, lambda_init=lambda_init),
        name="diff_attn", grid=(bsz, n_heads, nk),
        in_specs=[pl.BlockSpec(memory_space=pltpu.SMEM),
                  pl.BlockSpec((4, DIFF_HEAD_DIM), lambda b, h, i: (0, 0)),
                  pl.BlockSpec((1, hw), lambda b, h, i: (0, 0)),
                  pl.BlockSpec((t, hw), lambda b, h, i: (row0 + b * nk + i, qb + h)),
                  pl.BlockSpec((seq, hw), lambda b, h, i: (seq0 + b, qb + n_heads + h),
                               pipeline_mode=pl.Buffered(1)),
                  pl.BlockSpec((seq, hw), lambda b, h, i: (seq0 + b, qb + 2 * n_heads + h),
                               pipeline_mode=pl.Buffered(1)),
                  pl.BlockSpec((1, 3, t, t), lambda b, h, i: (h, 0, 0, 0))],
        out_specs=pl.BlockSpec((t, hw), lambda b, h, i: (b * nk + i, h)),
        out_shape=jax.ShapeDtypeStruct((bsz * seq, n_heads * hw), BF16),
        scratch_shapes=[pltpu.VMEM((2, t, LANES), F32), pltpu.VMEM((2, t, LANES), F32),
                        pltpu.VMEM((2, t, hw), F32), pltpu.VMEM((2, t, FAR_W * t), F32),
                        pltpu.VMEM((2, t, FAR_W * t), F32)],
        compiler_params=_cparams(("parallel", "parallel", "arbitrary")))(
            far, lambda_qk.astype(F32), subln_w.reshape(1, hw).astype(F32), proj, proj, proj, bias_near)


def _gelu_tanh(x):
    return 0.5 * x * (1.0 + jnp.tanh(math.sqrt(2.0 / math.pi) * (x + 0.044715 * (x * x * x))))


def _gmlp_kernel(u_ref, v_ref, lng_ref, lnb_ref, ws_ref, bs_ref, o_ref, *, rows, groups):
    u = _gelu_tanh(u_ref[...].astype(F32))
    v = _gelu_tanh(v_ref[...].astype(F32))
    mu = jnp.mean(v, axis=-1, keepdims=True)
    vc = v - mu
    vn = (vc * lax.rsqrt(jnp.mean(vc * vc, axis=-1, keepdims=True) + LN_EPS) * lng_ref[...] + lnb_ref[...]).astype(BF16)
    gd = v.shape[1] // groups
    L = GMLP_CHUNK
    for c in range(rows // L):
        for g in range(groups):
            mixed = _dot(ws_ref[g], vn[c * L:(c + 1) * L, g * gd:(g + 1) * gd]) + bs_ref[:, g * gd:(g + 1) * gd]
            o_ref[c * L:(c + 1) * L, g * gd:(g + 1) * gd] = (
                u[c * L:(c + 1) * L, g * gd:(g + 1) * gd] * mixed).astype(o_ref.dtype)


def _gmlp(proj, u_off, width, ln_v, w_s, b_s):
    t = proj.shape[0]
    groups = w_s.shape[0]
    L = GMLP_CHUNK
    rows = min(GMLP_ROWS, t)
    assert u_off % width == 0 and rows % L == 0
    ub = u_off // width
    bias_x = jnp.repeat(b_s.astype(F32).T, width // groups, axis=1)
    vec = pl.BlockSpec((1, width), lambda i: (0, 0))
    return pl.pallas_call(
        functools.partial(_gmlp_kernel, rows=rows, groups=groups), name="gmlp", grid=(t // rows,),
        in_specs=[pl.BlockSpec((rows, width), lambda i: (i, ub)),
                  pl.BlockSpec((rows, width), lambda i: (i, ub + 1)),
                  vec, vec,
                  pl.BlockSpec((groups, L, L), lambda i: (0, 0, 0)),
                  pl.BlockSpec((L, width), lambda i: (0, 0))],
        out_specs=pl.BlockSpec((rows, width), lambda i: (i, 0)),
        out_shape=jax.ShapeDtypeStruct((t, width), BF16),
        compiler_params=_cparams(("parallel",)))(
            proj, proj, ln_v[0].reshape(1, width).astype(F32), ln_v[1].reshape(1, width).astype(F32),
            w_s.astype(BF16), bias_x)


def _merge_kernel(h_ref, s_ref, a0_ref, a1_ref, m_ref, wg0_ref, wg1_ref, wg2_ref, wos_ref, woa_ref, wom_ref,
                  o_ref, *, first_blocks):
    def gated_sum(a_ref):
        h = h_ref[...]
        merged = (jax.nn.sigmoid(_dot(h, wg0_ref[...])) * _dot(s_ref[...], wos_ref[...])
                  + jax.nn.sigmoid(_dot(h, wg1_ref[...])) * _dot(a_ref[...], woa_ref[...])
                  + jax.nn.sigmoid(_dot(h, wg2_ref[...])) * _dot(m_ref[...], wom_ref[...]))
        o_ref[...] = merged.astype(o_ref.dtype)

    in_first = pl.program_id(0) < first_blocks
    pl.when(in_first)(lambda: gated_sum(a0_ref))
    pl.when(jnp.logical_not(in_first))(lambda: gated_sum(a1_ref))


def _merge(h, y_s, y_a_parts, y_m, w_gate, w_os, w_oa, w_om):
    t, d = h.shape
    ya0, ya1 = y_a_parts
    tm = min(MERGE_TM, ya0.shape[0], ya1.shape[0])
    tn = min(MERGE_TN, d)
    nj = d // tn
    na = ya0.shape[0] // tm
    assert ya0.shape[0] % tm == 0 and ya1.shape[0] % tm == 0
    act = lambda a: pl.BlockSpec((tm, a.shape[1]), lambda i, j: (i, 0))
    wcol = lambda w, off: pl.BlockSpec((w.shape[0], tn), lambda i, j: (0, j + off))
    return pl.pallas_call(
        functools.partial(_merge_kernel, first_blocks=na), name="merge", grid=(t // tm, nj),
        in_specs=[act(h), act(y_s),
                  pl.BlockSpec((tm, ya0.shape[1]), lambda i, j: (jnp.minimum(i, na - 1), 0)),
                  pl.BlockSpec((tm, ya1.shape[1]), lambda i, j: (jnp.maximum(i - na, 0), 0)),
                  act(y_m),
                  wcol(w_gate, 0), wcol(w_gate, nj), wcol(w_gate, 2 * nj),
                  wcol(w_os, 0), wcol(w_oa, 0), wcol(w_om, 0)],
        out_specs=pl.BlockSpec((tm, tn), lambda i, j: (i, j)),
        out_shape=jax.ShapeDtypeStruct((t, d), BF16),
        compiler_params=_cparams(("parallel", "arbitrary")))(
            h, y_s, ya0, ya1, y_m, w_gate, w_gate, w_gate, w_os, w_oa, w_om)


def kernel(x_prompt, x_sample, rel_bias, norms, w_ffn1_in, w_ffn1_out, w_in, conv_w, conv_b, dt_bias, a_log,
           d_skip, ssm_norm, w_o_ssm, lambda_qk, diff_subln, w_o_diff, ln_v, w_spatial, b_spatial, w_o_gmlp,
           w_out, w_ffn2_in, w_ffn2_out):
    depth = norms.shape[0]
    d = x_prompt.shape[-1]
    d_inner = w_o_ssm.shape[1]
    conv_ch = conv_w.shape[2]
    heads = d_skip.shape[1]
    groups = (conv_ch - d_inner) // (2 * SSM_STATE)
    diff_w = w_o_diff.shape[1]
    diff_heads = diff_w // (2 * DIFF_HEAD_DIM)
    gmlp_w = w_o_gmlp.shape[1]
    pb, ps = x_prompt.shape[:2]
    sb, ss = x_sample.shape[:2]
    seq_lens = [ps] * pb + [ss] * sb
    n_prompt = pb * ps

    c_dt = d_inner + conv_ch
    c_diff = c_dt + 2 * heads
    c_gmlp = c_diff + 3 * diff_w
    c_gate = c_gmlp + 2 * gmlp_w
    q_off = c_dt
    u_off = q_off + 3 * diff_w
    n_main = u_off + 2 * gmlp_w
    col_scale = jnp.ones((1, n_main), F32).at[:, q_off:q_off + diff_w].set(DIFF_HEAD_DIM ** -0.5 * LOG2E)

    x = [x_prompt.reshape(n_prompt, d), x_sample.reshape(sb * ss, d)]
    h = _rmsnorm(x, norms[0, 0])
    for l in range(depth):
        n = norms[l]
        lambda_init = 0.8 - 0.6 * math.exp(-0.3 * l)
        wi = w_in[l]
        w_main = jnp.concatenate([wi[:, :c_dt], wi[:, c_diff:c_gate]], axis=1).astype(BF16)
        w_dt = jnp.pad(wi[:, c_dt:c_diff], ((0, 0), (0, LANES - 2 * heads))).astype(BF16)
        w_gate = wi[:, c_gate:].astype(BF16)

        y = _matmul(_matmul_swiglu(h, w_ffn1_in[l].astype(BF16)), w_ffn1_out[l].astype(BF16), BF16, tn=FFN_OUT_TN)
        x, h = _resid_norm(x, y, n[1], 0.5, n[2])
        x = [x]

        proj = _matmul(h, w_main, BF16, scale=col_scale)
        dt_raw = _matmul(h, w_dt, F32, tn=LANES)
        xbc = _conv_silu(proj, d_inner, conv_w[l], conv_b[l], seq_lens)
        y_ssm = _ssd(xbc, proj, dt_raw, dt_bias[l], a_log[l], d_skip[l], ssm_norm[l], seq_lens, groups)
        y_att = [
            _diff_attention(proj, q_off, diff_heads, rel_bias, lambda_qk[l], diff_subln[l], lambda_init, 0, pb, ps),
            _diff_attention(proj, q_off, diff_heads, rel_bias, lambda_qk[l], diff_subln[l], lambda_init,
                            n_prompt, sb, ss)]
        y_gmlp = _gmlp(proj, u_off, gmlp_w, ln_v[l], w_spatial[l], b_spatial[l])
        merged = _merge(h, y_ssm, y_att, y_gmlp, w_gate, w_o_ssm[l].astype(BF16), w_o_diff[l].astype(BF16),
                        w_o_gmlp[l].astype(BF16))
        y = _matmul(merged, w_out[l].astype(BF16), BF16)
        x, h = _resid_norm(x, y, n[3], 1.0, n[4])
        x = [x]

        y = _matmul(_matmul_swiglu(h, w_ffn2_in[l].astype(BF16)), w_ffn2_out[l].astype(BF16), BF16, tn=FFN_OUT_TN)
        if l + 1 < depth:
            x, h = _resid_norm(x, y, n[5], 0.5, norms[l + 1, 0])
            x = [x]

    y_prompt = _resid_rows(x[0], y, norms[depth - 1, 5], 0.5, 0, n_prompt)
    y_sample = _resid_rows(x[0], y, norms[depth - 1, 5], 0.5, n_prompt, sb * ss)
    return y_prompt.reshape(x_prompt.shape), y_sample.reshape(x_sample.shape)
```

```python
import functools
import math

import jax
import jax.numpy as jnp
from jax import lax
from jax.experimental import pallas as pl
from jax.experimental.pallas import tpu as pltpu

F32 = jnp.float32
BF16 = jnp.bfloat16

SSM_HEAD_DIM = 64
SSM_STATE = 128
SSD_CHUNK = 128
DIFF_HEAD_DIM = 128
NUM_BUCKETS = 32
MAX_DISTANCE = 128
GMLP_CHUNK = 128
NORM_EPS = 1e-6
SUBLN_EPS = 1e-5
LN_EPS = 1e-5
LOG2E = math.log2(math.e)

LANES = 128
BF16_SUBLANES = 16
VMEM_LIMIT_BYTES = 56 * 1024 * 1024

MM_TM = 1024
MM_TN = 1024
FFN_OUT_TN = 2048
ROW_TILE = 256
CONV_ROWS = 1024
CONV_COLS = 1024
CONV_BLOCK = 128
ATTN_TILE = 512
FAR_W = 4
MERGE_TM = 512
MERGE_TN = 512
MERGE_VMEM_LIMIT_BYTES = 58 * 1024 * 1024
GMLP_ROWS = 512


def _cparams(sem, vmem_limit_bytes=VMEM_LIMIT_BYTES):
    return pltpu.CompilerParams(dimension_semantics=sem, vmem_limit_bytes=vmem_limit_bytes)


def _dot(a, b):
    return jnp.dot(a, b, preferred_element_type=F32)


def _dot_nt(a, b):
    return lax.dot_general(a, b, (((1,), (1,)), ((), ())), preferred_element_type=F32)


def _dot_tn(a, b):
    return lax.dot_general(a, b, (((0,), (0,)), ((), ())), preferred_element_type=F32)


def _split_bf16(x, n):
    parts = []
    r = x
    for _ in range(n):
        p = r.astype(BF16)
        parts.append(p)
        r = r - p.astype(F32)
    return parts


def _rms(x, eps):
    return x * lax.rsqrt(jnp.mean(x * x, axis=-1, keepdims=True) + eps)


def _silu(x):
    return x * jax.nn.sigmoid(x)


def _softplus(x):
    return jnp.maximum(x, 0.0) + jnp.log1p(jnp.exp(-jnp.abs(x)))


def _mm_scale_kernel(a_ref, w_ref, s_ref, o_ref):
    o_ref[...] = (_dot(a_ref[...], w_ref[...]) * s_ref[...]).astype(o_ref.dtype)


def _mm_kernel(a_ref, w_ref, o_ref):
    o_ref[...] = _dot(a_ref[...], w_ref[...]).astype(o_ref.dtype)


def _mm_swiglu_kernel(a_ref, wg_ref, wu_ref, o_ref):
    a = a_ref[...]
    g = _dot(a, wg_ref[...])
    u = _dot(a, wu_ref[...])
    o_ref[...] = (_silu(g) * u).astype(o_ref.dtype)


def _matmul(a, w, out_dtype, scale=None, tm=None, tn=None):
    m, k = a.shape
    n = w.shape[1]
    tm = min(tm or MM_TM, m)
    tn = min(tn or MM_TN, n)
    in_specs = [pl.BlockSpec((tm, k), lambda i, j: (i, 0)), pl.BlockSpec((k, tn), lambda i, j: (0, j))]
    args = [a, w]
    body = _mm_kernel
    if scale is not None:
        in_specs.append(pl.BlockSpec((1, tn), lambda i, j: (0, j)))
        args.append(scale)
        body = _mm_scale_kernel
    return pl.pallas_call(
        body, name="mm", grid=(m // tm, n // tn), in_specs=in_specs,
        out_specs=pl.BlockSpec((tm, tn), lambda i, j: (i, j)),
        out_shape=jax.ShapeDtypeStruct((m, n), out_dtype),
        compiler_params=_cparams(("parallel", "arbitrary")))(*args)


def _matmul_swiglu(a, w):
    m, k = a.shape
    n = w.shape[1] // 2
    tm = min(MM_TM, m)
    tn = min(MM_TN // 2, n)
    nj = n // tn
    return pl.pallas_call(
        _mm_swiglu_kernel, name="mm_swiglu", grid=(m // tm, nj),
        in_specs=[pl.BlockSpec((tm, k), lambda i, j: (i, 0)),
                  pl.BlockSpec((k, tn), lambda i, j: (0, j)),
                  pl.BlockSpec((k, tn), lambda i, j: (0, j + nj))],
        out_specs=pl.BlockSpec((tm, tn), lambda i, j: (i, j)),
        out_shape=jax.ShapeDtypeStruct((m, n), BF16),
        compiler_params=_cparams(("parallel", "arbitrary")))(a, w, w)


def _stream_specs(parts, tr):
    d = parts[0].shape[1]
    if len(parts) == 1:
        return [pl.BlockSpec((tr, d), lambda i: (i, 0))], None
    na = parts[0].shape[0] // tr
    return [pl.BlockSpec((tr, d), lambda i: (jnp.minimum(i, na - 1), 0)),
            pl.BlockSpec((tr, d), lambda i: (jnp.maximum(i - na, 0), 0))], na


def _stream_block(x_refs, first_blocks):
    if len(x_refs) == 1:
        return x_refs[0][...]
    return jnp.where(pl.program_id(0) < first_blocks, x_refs[0][...], x_refs[1][...])


def _rmsnorm_kernel(*refs, n_parts, first_blocks):
    g_ref, o_ref = refs[n_parts:]
    o_ref[...] = (_rms(_stream_block(refs[:n_parts], first_blocks), NORM_EPS) * g_ref[...]).astype(o_ref.dtype)


def _rmsnorm(parts, g):
    d = parts[0].shape[1]
    t = sum(p.shape[0] for p in parts)
    tr = min(ROW_TILE, min(p.shape[0] for p in parts))
    x_specs, first_blocks = _stream_specs(parts, tr)
    return pl.pallas_call(
        functools.partial(_rmsnorm_kernel, n_parts=len(parts), first_blocks=first_blocks),
        name="rmsnorm", grid=(t // tr,),
        in_specs=x_specs + [pl.BlockSpec((1, d), lambda i: (0, 0))],
        out_specs=pl.BlockSpec((tr, d), lambda i: (i, 0)),
        out_shape=jax.ShapeDtypeStruct((t, d), BF16),
        compiler_params=_cparams(("parallel",)))(*parts, g.reshape(1, d))


def _resid_norm_kernel(*refs, n_parts, first_blocks, scale):
    y_ref, g1_ref, g2_ref, xo_ref, h_ref = refs[n_parts:]
    xn = _stream_block(refs[:n_parts], first_blocks) + scale * (_rms(y_ref[...].astype(F32), NORM_EPS) * g1_ref[...])
    xo_ref[...] = xn
    h_ref[...] = (_rms(xn, NORM_EPS) * g2_ref[...]).astype(h_ref.dtype)


def _resid_kernel(x_ref, y_ref, g1_ref, xo_ref, *, scale):
    xo_ref[...] = x_ref[...] + scale * (_rms(y_ref[...].astype(F32), NORM_EPS) * g1_ref[...])


def _resid_norm(parts, y, g_post, scale, g_next):
    t, d = y.shape
    tr = min(ROW_TILE, min(p.shape[0] for p in parts))
    x_specs, first_blocks = _stream_specs(parts, tr)
    row = pl.BlockSpec((tr, d), lambda i: (i, 0))
    vec = pl.BlockSpec((1, d), lambda i: (0, 0))
    return pl.pallas_call(
        functools.partial(_resid_norm_kernel, n_parts=len(parts), first_blocks=first_blocks, scale=scale),
        name="resid_norm", grid=(t // tr,),
        in_specs=x_specs + [row, vec, vec], out_specs=[row, row],
        out_shape=[jax.ShapeDtypeStruct((t, d), F32), jax.ShapeDtypeStruct((t, d), BF16)],
        compiler_params=_cparams(("parallel",)))(*parts, y, g_post.reshape(1, d), g_next.reshape(1, d))


def _resid_rows(x, y, g_post, scale, row_off, n_rows):
    d = x.shape[1]
    tr = min(ROW_TILE, n_rows)
    assert row_off % tr == 0 and n_rows % tr == 0
    b0 = row_off // tr
    row = pl.BlockSpec((tr, d), lambda i: (b0 + i, 0))
    return pl.pallas_call(
        functools.partial(_resid_kernel, scale=scale), name="resid", grid=(n_rows // tr,),
        in_specs=[row, row, pl.BlockSpec((1, d), lambda i: (0, 0))],
        out_specs=pl.BlockSpec((tr, d), lambda i: (i, 0)),
        out_shape=jax.ShapeDtypeStruct((n_rows, d), F32),
        compiler_params=_cparams(("parallel",)))(x, y, g_post.reshape(1, d))


def _conv_kernel(prev_ref, x_ref, next_ref, w_ref, b_ref, sh_ref, o_ref, *, rows, width, start_tiles, end_tiles):
    i = pl.program_id(0)
    halo = BF16_SUBLANES
    pad = width // 2
    blk = CONV_BLOCK
    is_start = functools.reduce(jnp.logical_or, [i == s for s in start_tiles])
    is_end = functools.reduce(jnp.logical_or, [i == e for e in end_tiles])
    prev = prev_ref[...]
    nxt = next_ref[...]
    ext = jnp.concatenate([jnp.where(is_start, jnp.zeros_like(prev), prev), x_ref[...],
                           jnp.where(is_end, jnp.zeros_like(nxt), nxt)], axis=0)
    w = w_ref[...]
    bias = b_ref[...]
    select = sh_ref[...]
    taps = [k for k in range(width) if k != pad]
    for rb in range(rows // blk):
        shifted = _dot(select, ext[rb * blk:rb * blk + blk + 2 * halo, :])
        acc = bias + w[pad:pad + 1, :] * x_ref[rb * blk:(rb + 1) * blk, :].astype(F32)
        for n, k in enumerate(taps):
            acc = acc + w[k:k + 1, :] * shifted[n * blk:(n + 1) * blk, :]
        o_ref[rb * blk:(rb + 1) * blk, :] = _silu(acc).astype(o_ref.dtype)


def _conv_silu(proj, col_off, conv_w, conv_b, seq_lens):
    t = proj.shape[0]
    width, ch = conv_w.shape
    rows = min(CONV_ROWS, min(seq_lens))
    cols = min(CONV_COLS, ch)
    halo = BF16_SUBLANES
    blk = CONV_BLOCK
    pad = width // 2
    assert col_off % cols == 0 and ch % cols == 0 and all(s % rows == 0 for s in seq_lens)
    assert rows % blk == 0 and pad <= halo
    starts, ends, pos = [], [], 0
    for s in seq_lens:
        starts.append(pos // rows)
        pos += s
        ends.append(pos // rows - 1)
    rb = rows // halo
    nhalo = t // halo
    cb = col_off // cols
    taps = jnp.array([k for k in range(width) if k != pad])
    src = jnp.arange(blk)[None, :] + halo + (taps - pad)[:, None]
    select = (src.reshape(-1)[:, None] == jnp.arange(blk + 2 * halo)[None, :]).astype(BF16)
    return pl.pallas_call(
        functools.partial(_conv_kernel, rows=rows, width=width, start_tiles=tuple(starts), end_tiles=tuple(ends)),
        name="conv_silu", grid=(t // rows, ch // cols),
        in_specs=[pl.BlockSpec((halo, cols), lambda i, j: (jnp.maximum(i * rb - 1, 0), cb + j)),
                  pl.BlockSpec((rows, cols), lambda i, j: (i, cb + j)),
                  pl.BlockSpec((halo, cols), lambda i, j: (jnp.minimum((i + 1) * rb, nhalo - 1), cb + j)),
                  pl.BlockSpec((width, cols), lambda i, j: (0, j)),
                  pl.BlockSpec((1, cols), lambda i, j: (0, j)),
                  pl.BlockSpec(select.shape, lambda i, j: (0, 0))],
        out_specs=pl.BlockSpec((rows, cols), lambda i, j: (i, j)),
        out_shape=jax.ShapeDtypeStruct((t, ch), BF16),
        compiler_params=_cparams(("parallel", "arbitrary")))(
            proj, proj, proj, conv_w, conv_b.reshape(1, ch), select)


def _ssd_kernel(*refs, reverse, n_chunks, reset_chunks, groups, heads, d_inner):
    if reverse:
        (xbc_ref, dt_ref, dtb_ref, alog_ref, tri_ref, trit_ref, exp_ref,
         yf_ref, z_ref, nw_ref, o_ref, state_ref) = refs
    else:
        (xbc_ref, dt_ref, dtb_ref, alog_ref, tri_ref, trit_ref, exp_ref,
         dskip_ref, o_ref, state_ref) = refs
    step = pl.program_id(0)
    chunk = (n_chunks - 1 - step) if reverse else step
    L = SSD_CHUNK
    gw = d_inner // groups
    hpg = heads // groups
    n = SSM_STATE
    col0 = heads if reverse else 0

    @pl.when(functools.reduce(jnp.logical_or, [chunk == c for c in reset_chunks]))
    def _():
        state_ref[...] = jnp.zeros_like(state_ref)

    dt = _softplus(dt_ref[...] + dtb_ref[...])
    dta = dt * (-jnp.exp(alog_ref[...]))
    dt_t = dt.T
    dta_t = dta.T
    tri = tri_ref[...]
    tri_t = trit_ref[...]
    acum = sum(_dot(tri, p) for p in _split_bf16(dta, 3))
    acum_t = sum(_dot(p, tri_t) for p in _split_bf16(dta_t, 3))
    edge = 0 if reverse else L - 1
    a_end = acum[edge:edge + 1, :]
    to_end = jnp.exp(a_end - acum) * dt
    ea = jnp.exp(acum)
    expand = exp_ref[...]
    to_end_x = sum(_dot(p, expand) for p in _split_bf16(to_end, 2))
    ea_x = sum(_dot(p, expand) for p in _split_bf16(ea, 2))
    decay_x = ea_x[edge:edge + 1, :]
    acum2 = acum * LOG2E
    row_t = acum_t * LOG2E - jnp.log2(dt_t)

    row = lax.broadcasted_iota(jnp.int32, (L, L), 0)
    colm = lax.broadcasted_iota(jnp.int32, (L, L), 1)
    causal = (row <= colm) if reverse else (row >= colm)
    lane = lax.broadcasted_iota(jnp.int32, (L, LANES), 1)
    first_half = lane < SSM_HEAD_DIM

    for g in range(groups):
        xs = xbc_ref[:, g * gw:(g + 1) * gw]
        bm = xbc_ref[:, d_inner + g * n:d_inner + (g + 1) * n]
        cm = xbc_ref[:, d_inner + groups * n + g * n:d_inner + groups * n + (g + 1) * n]
        cb = _dot_nt(cm, bm)
        st = state_ref[g]
        y_off = _dot(cm, st.astype(BF16)) * ea_x[:, g * gw:(g + 1) * gw]
        xs_scaled = (xs.astype(F32) * to_end_x[:, g * gw:(g + 1) * gw]).astype(BF16)
        state_ref[g] = st * decay_x[:, g * gw:(g + 1) * gw] + _dot_tn(bm, xs_scaled)
        pairs = []
        for k in range(hpg // 2):
            xp = xs[:, k * LANES:(k + 1) * LANES]
            y_pair = None
            for side in range(2):
                c = col0 + g * hpg + 2 * k + side
                seg = acum2[:, c:c + 1] - row_t[c:c + 1, :]
                m = (cb * jnp.exp2(jnp.where(causal, seg, -jnp.inf))).astype(BF16)
                keep = first_half if side == 0 else jnp.logical_not(first_half)
                part = _dot(m, jnp.where(keep, xp, jnp.zeros_like(xp)))
                y_pair = part if y_pair is None else y_pair + part
            lo = g * gw + k * LANES
            y_pair = y_pair + y_off[:, k * LANES:(k + 1) * LANES]
            if not reverse:
                o_ref[:, lo:lo + LANES] = y_pair + dskip_ref[:, lo:lo + LANES] * xp.astype(F32)
            else:
                pairs.append(y_pair)
        if reverse:
            y = jnp.concatenate(pairs, axis=1) + yf_ref[:, g * gw:(g + 1) * gw]
            y = y * _silu(z_ref[:, g * gw:(g + 1) * gw].astype(F32))
            o_ref[:, g * gw:(g + 1) * gw] = (_rms(y, NORM_EPS) * nw_ref[:, g * gw:(g + 1) * gw]).astype(o_ref.dtype)


def _ssd(xbc, proj, dt_raw, dt_bias, a_log, d_skip, norm_w, seq_lens, groups):
    t = xbc.shape[0]
    heads = d_skip.shape[0]
    d_inner = heads * SSM_HEAD_DIM
    L = SSD_CHUNK
    nc = t // L
    assert 2 * heads <= LANES and (heads // groups) % 2 == 0
    starts, lasts, pos = [], [], 0
    for s in seq_lens:
        assert s % L == 0
        starts.append(pos // L)
        pos += s
        lasts.append(pos // L - 1)
    padh = LANES - 2 * heads
    dtb = jnp.pad(dt_bias.reshape(-1), (0, padh))
    alog = jnp.pad(a_log.reshape(-1), (0, padh))
    tri_lo = jnp.tril(jnp.ones((L, L), F32)).astype(BF16)
    tri_up = jnp.triu(jnp.ones((L, L), F32)).astype(BF16)
    head_of_col = jnp.arange(d_inner) // SSM_HEAD_DIM
    dskip_x = jnp.repeat(d_skip.astype(F32), SSM_HEAD_DIM).reshape(1, d_inner)

    def run(reverse, extra_args, extra_specs, out_dtype):
        cidx = (lambda i: nc - 1 - i) if reverse else (lambda i: i)
        tri = tri_up if reverse else tri_lo
        expand = (jnp.arange(LANES)[:, None] == (head_of_col + (heads if reverse else 0))[None, :]).astype(BF16)
        const = lambda shape: pl.BlockSpec(shape, lambda i: (0, 0))
        in_specs = [pl.BlockSpec((L, xbc.shape[1]), lambda i: (cidx(i), 0)),
                    pl.BlockSpec((L, LANES), lambda i: (cidx(i), 0)),
                    const((1, LANES)), const((1, LANES)),
                    const((L, L)), const((L, L)), const((LANES, d_inner))] + extra_specs(cidx)
        return pl.pallas_call(
            functools.partial(_ssd_kernel, reverse=reverse, n_chunks=nc,
                              reset_chunks=tuple(lasts if reverse else starts),
                              groups=groups, heads=heads, d_inner=d_inner),
            name="ssd_bwd" if reverse else "ssd_fwd", grid=(nc,), in_specs=in_specs,
            out_specs=pl.BlockSpec((L, d_inner), lambda i: (cidx(i), 0)),
            out_shape=jax.ShapeDtypeStruct((t, d_inner), out_dtype),
            scratch_shapes=[pltpu.VMEM((groups, SSM_STATE, d_inner // groups), F32)],
            compiler_params=_cparams(("arbitrary",)))(
                xbc, dt_raw, dtb.reshape(1, LANES), alog.reshape(1, LANES), tri, tri.T, expand, *extra_args)

    y_fwd = run(False, [dskip_x], lambda cidx: [pl.BlockSpec((1, d_inner), lambda i: (0, 0))], F32)
    return run(True, [y_fwd, proj, norm_w.reshape(1, d_inner)],
               lambda cidx: [pl.BlockSpec((L, d_inner), lambda i: (cidx(i), 0)),
                             pl.BlockSpec((L, d_inner), lambda i: (cidx(i), 0)),
                             pl.BlockSpec((1, d_inner), lambda i: (0, 0))], BF16)


def _t5_bucket(rel):
    half = NUM_BUCKETS // 2
    max_exact = half // 2
    n = jnp.abs(rel)
    log_ratio = jnp.log(jnp.maximum(n, 1).astype(F32) / max_exact) / math.log(MAX_DISTANCE / max_exact)
    large = jnp.minimum(max_exact + (log_ratio * (half - max_exact)).astype(jnp.int32), half - 1)
    return jnp.where(rel > 0, half, 0) + jnp.where(n < max_exact, n, large)


def _attn_kernel(far_ref, lq_ref, sw_ref, q_ref, k_ref, v_ref, bias_ref, o_ref, m_ref, l_ref, acc_ref,
                 sa_ref, sb_ref, *, t, nk, lambda_init):
    h = pl.program_id(1)
    i = pl.program_id(2)
    dh = DIFF_HEAD_DIM
    nblk = t // LANES
    c_before = far_ref[h, 0]
    c_after = far_ref[h, 1]
    m_ref[...] = jnp.full(m_ref.shape, -jnp.inf, F32)
    l_ref[...] = jnp.zeros(l_ref.shape, F32)
    acc_ref[...] = jnp.zeros(acc_ref.shape, F32)

    def qk(kb, j, width=1):
        ks = k_ref[pl.ds(pl.multiple_of(kb * t, t), width * t), j * dh:(j + 1) * dh]
        return _dot_nt(q_ref[:, j * dh:(j + 1) * dh], ks)

    def softmax_pv(kb, s_of, shift, width=1):
        vs = v_ref[pl.ds(pl.multiple_of(kb * t, t), width * t), :]
        for j in range(2):
            s = s_of(j)
            blocks = [s[:, c * LANES:(c + 1) * LANES] for c in range(width * nblk)]
            row_max = jnp.max(functools.reduce(jnp.maximum, blocks), axis=-1, keepdims=True)
            m_old = m_ref[j]
            m_new = jnp.maximum(m_old, row_max + shift)
            alpha = jnp.exp2(m_old - m_new)
            ref = m_new - shift
            ps = [jnp.exp2(b - ref) for b in blocks]
            l_ref[j] = alpha * l_ref[j] + functools.reduce(jnp.add, ps)
            p = jnp.concatenate([x.astype(BF16) for x in ps], axis=1)
            acc_ref[j] = jnp.concatenate([alpha] * (acc_ref.shape[2] // LANES), axis=1) * acc_ref[j] + _dot(p, vs)
            m_ref[j] = m_new

    def update(kb, bias_tile):
        softmax_pv(kb, lambda j: qk(kb, j) + bias_tile, 0.0)

    def far_loop(lo, hi, shift):
        w = FAR_W
        n = hi - lo
        nd = n // w
        base = lo + n % w

        def single(kb, carry):
            softmax_pv(kb, lambda j: qk(kb, j), shift)
            return carry

        lax.fori_loop(lo, base, single, 0)

        @pl.when(nd > 0)
        def _():
            for j in range(2):
                sa_ref[j] = qk(base, j, w)

            def pair(kp, carry):
                d0 = base + 2 * w * kp
                d1 = jnp.minimum(d0 + w, hi - w)
                d2 = jnp.minimum(d0 + 2 * w, hi - w)
                for j in range(2):
                    sb_ref[j] = qk(d1, j, w)
                softmax_pv(d0, lambda j: sa_ref[j], shift, w)
                for j in range(2):
                    sa_ref[j] = qk(d2, j, w)
                softmax_pv(d1, lambda j: sb_ref[j], shift, w)
                return carry

            lax.fori_loop(0, nd // 2, pair, 0)

            @pl.when(nd % 2 == 1)
            def _():
                softmax_pv(hi - w, lambda j: sa_ref[j], shift, w)

    far_loop(0, jnp.maximum(i - 1, 0), c_before)

    @pl.when(i >= 1)
    def _():
        update(i - 1, bias_ref[0, 0])

    update(i, bias_ref[0, 1])

    @pl.when(i + 1 < nk)
    def _():
        update(i + 1, bias_ref[0, 2])

    far_loop(jnp.minimum(i + 2, nk), nk, c_after)

    lq = lq_ref[...]
    lam = (jnp.exp(jnp.sum(lq[0:1] * lq[1:2], axis=-1, keepdims=True))
           - jnp.exp(jnp.sum(lq[2:3] * lq[3:4], axis=-1, keepdims=True)) + lambda_init)
    l0 = jnp.sum(l_ref[0], axis=-1, keepdims=True)
    l1 = jnp.sum(l_ref[1], axis=-1, keepdims=True)
    o = acc_ref[0] / l0 - lam * (acc_ref[1] / l1)
    o_ref[...] = (_rms(o, SUBLN_EPS) * sw_ref[...] * (1.0 - lambda_init)).astype(o_ref.dtype)


def _near_bias_tiles(table_t, t):
    x = jnp.concatenate([jnp.arange(t), jnp.arange(t) - t])
    rel = (jnp.arange(-1, 2) * t)[:, None] + x[None, :]
    w = table_t[:, _t5_bucket(rel)]
    flat = jnp.tile(w, (1, 1, t))[..., :t * (2 * t - 1)]
    return flat.reshape(w.shape[0], 3, t, 2 * t - 1)[..., :t]


def _diff_attention(proj, q_off, n_heads, rel_bias, lambda_qk, subln_w, lambda_init, tok_off, bsz, seq):
    t = min(ATTN_TILE, seq)
    nk = seq // t
    hw = 2 * DIFF_HEAD_DIM
    assert t >= MAX_DISTANCE and seq % t == 0 and tok_off % seq == 0 and q_off % hw == 0
    qb = q_off // hw
    table_t = rel_bias.astype(F32).T * LOG2E
    bias_near = _near_bias_tiles(table_t, t)
    far = table_t[:, _t5_bucket(jnp.array([-2 * MAX_DISTANCE, 2 * MAX_DISTANCE]))]
    row0 = tok_off // t
    seq0 = tok_off // seq
    return pl.pallas_call(
        functools.partial(_attn_kernel, t=t, nk=nk, lambda_init=lambda_init),
        name="diff_attn", grid=(bsz, n_heads, nk),
        in_specs=[pl.BlockSpec(memory_space=pltpu.SMEM),
                  pl.BlockSpec((4, DIFF_HEAD_DIM), lambda b, h, i: (0, 0)),
                  pl.BlockSpec((1, hw), lambda b, h, i: (0, 0)),
                  pl.BlockSpec((t, hw), lambda b, h, i: (row0 + b * nk + i, qb + h)),
                  pl.BlockSpec((seq, hw), lambda b, h, i: (seq0 + b, qb + n_heads + h),
                               pipeline_mode=pl.Buffered(1)),
                  pl.BlockSpec((seq, hw), lambda b, h, i: (seq0 + b, qb + 2 * n_heads + h),
                               pipeline_mode=pl.Buffered(1)),
                  pl.BlockSpec((1, 3, t, t), lambda b, h, i: (h, 0, 0, 0))],
        out_specs=pl.BlockSpec((t, hw), lambda b, h, i: (b * nk + i, h)),
        out_shape=jax.ShapeDtypeStruct((bsz * seq, n_heads * hw), BF16),
        scratch_shapes=[pltpu.VMEM((2, t, LANES), F32), pltpu.VMEM((2, t, LANES), F32),
                        pltpu.VMEM((2, t, hw), F32), pltpu.VMEM((2, t, FAR_W * t), F32),
                        pltpu.VMEM((2, t, FAR_W * t), F32)],
        compiler_params=_cparams(("parallel", "parallel", "arbitrary")))(
            far, lambda_qk.astype(F32), subln_w.reshape(1, hw).astype(F32), proj, proj, proj, bias_near)


def _gelu_tanh(x):
    return 0.5 * x * (1.0 + jnp.tanh(math.sqrt(2.0 / math.pi) * (x + 0.044715 * (x * x * x))))


def _gmlp_kernel(u_ref, v_ref, lng_ref, lnb_ref, ws_ref, bs_ref, o_ref, *, rows, groups):
    u = _gelu_tanh(u_ref[...].astype(F32))
    v = _gelu_tanh(v_ref[...].astype(F32))
    mu = jnp.mean(v, axis=-1, keepdims=True)
    vc = v - mu
    vn = (vc * lax.rsqrt(jnp.mean(vc * vc, axis=-1, keepdims=True) + LN_EPS) * lng_ref[...] + lnb_ref[...]).astype(BF16)
    gd = v.shape[1] // groups
    L = GMLP_CHUNK
    for c in range(rows // L):
        for g in range(groups):
            mixed = _dot(ws_ref[g], vn[c * L:(c + 1) * L, g * gd:(g + 1) * gd]) + bs_ref[:, g * gd:(g + 1) * gd]
            o_ref[c * L:(c + 1) * L, g * gd:(g + 1) * gd] = (
                u[c * L:(c + 1) * L, g * gd:(g + 1) * gd] * mixed).astype(o_ref.dtype)


def _gmlp(proj, u_off, width, ln_v, w_s, b_s):
    t = proj.shape[0]
    groups = w_s.shape[0]
    L = GMLP_CHUNK
    rows = min(GMLP_ROWS, t)
    assert u_off % width == 0 and rows % L == 0
    ub = u_off // width
    bias_x = jnp.repeat(b_s.astype(F32).T, width // groups, axis=1)
    vec = pl.BlockSpec((1, width), lambda i: (0, 0))
    return pl.pallas_call(
        functools.partial(_gmlp_kernel, rows=rows, groups=groups), name="gmlp", grid=(t // rows,),
        in_specs=[pl.BlockSpec((rows, width), lambda i: (i, ub)),
                  pl.BlockSpec((rows, width), lambda i: (i, ub + 1)),
                  vec, vec,
                  pl.BlockSpec((groups, L, L), lambda i: (0, 0, 0)),
                  pl.BlockSpec((L, width), lambda i: (0, 0))],
        out_specs=pl.BlockSpec((rows, width), lambda i: (i, 0)),
        out_shape=jax.ShapeDtypeStruct((t, width), BF16),
        compiler_params=_cparams(("parallel",)))(
            proj, proj, ln_v[0].reshape(1, width).astype(F32), ln_v[1].reshape(1, width).astype(F32),
            w_s.astype(BF16), bias_x)


def _merge_kernel(h_ref, s_ref, a0_ref, a1_ref, m_ref, wg0_ref, wg1_ref, wg2_ref, wos_ref, woa_ref, wom_ref,
                  o_ref, *, first_blocks):
    def gated_sum(a_ref):
        h = h_ref[...]
        merged = (jax.nn.sigmoid(_dot(h, wg0_ref[...])) * _dot(s_ref[...], wos_ref[...])
                  + jax.nn.sigmoid(_dot(h, wg1_ref[...])) * _dot(a_ref[...], woa_ref[...])
                  + jax.nn.sigmoid(_dot(h, wg2_ref[...])) * _dot(m_ref[...], wom_ref[...]))
        o_ref[...] = merged.astype(o_ref.dtype)

    in_first = pl.program_id(0) < first_blocks
    pl.when(in_first)(lambda: gated_sum(a0_ref))
    pl.when(jnp.logical_not(in_first))(lambda: gated_sum(a1_ref))


def _merge(h, y_s, y_a_parts, y_m, w_gate, w_os, w_oa, w_om):
    t, d = h.shape
    ya0, ya1 = y_a_parts
    tm = min(MERGE_TM, ya0.shape[0], ya1.shape[0])
    tn = min(MERGE_TN, d)
    nj = d // tn
    na = ya0.shape[0] // tm
    assert ya0.shape[0] % tm == 0 and ya1.shape[0] % tm == 0
    act = lambda a: pl.BlockSpec((tm, a.shape[1]), lambda i, j: (i, 0))
    wcol = lambda w, off: pl.BlockSpec((w.shape[0], tn), lambda i, j: (0, j + off))
    return pl.pallas_call(
        functools.partial(_merge_kernel, first_blocks=na), name="merge", grid=(t // tm, nj),
        in_specs=[act(h), act(y_s),
                  pl.BlockSpec((tm, ya0.shape[1]), lambda i, j: (jnp.minimum(i, na - 1), 0)),
                  pl.BlockSpec((tm, ya1.shape[1]), lambda i, j: (jnp.maximum(i - na, 0), 0)),
                  act(y_m),
                  wcol(w_gate, 0), wcol(w_gate, nj), wcol(w_gate, 2 * nj),
                  wcol(w_os, 0), wcol(w_oa, 0), wcol(w_om, 0)],
        out_specs=pl.BlockSpec((tm, tn), lambda i, j: (i, j)),
        out_shape=jax.ShapeDtypeStruct((t, d), BF16),
        compiler_params=_cparams(("parallel", "arbitrary"), MERGE_VMEM_LIMIT_BYTES))(
            h, y_s, ya0, ya1, y_m, w_gate, w_gate, w_gate, w_os, w_oa, w_om)


def kernel(x_prompt, x_sample, rel_bias, norms, w_ffn1_in, w_ffn1_out, w_in, conv_w, conv_b, dt_bias, a_log,
           d_skip, ssm_norm, w_o_ssm, lambda_qk, diff_subln, w_o_diff, ln_v, w_spatial, b_spatial, w_o_gmlp,
           w_out, w_ffn2_in, w_ffn2_out):
    depth = norms.shape[0]
    d = x_prompt.shape[-1]
    d_inner = w_o_ssm.shape[1]
    conv_ch = conv_w.shape[2]
    heads = d_skip.shape[1]
    groups = (conv_ch - d_inner) // (2 * SSM_STATE)
    diff_w = w_o_diff.shape[1]
    diff_heads = diff_w // (2 * DIFF_HEAD_DIM)
    gmlp_w = w_o_gmlp.shape[1]
    pb, ps = x_prompt.shape[:2]
    sb, ss = x_sample.shape[:2]
    seq_lens = [ps] * pb + [ss] * sb
    n_prompt = pb * ps

    c_dt = d_inner + conv_ch
    c_diff = c_dt + 2 * heads
    c_gmlp = c_diff + 3 * diff_w
    c_gate = c_gmlp + 2 * gmlp_w
    q_off = c_dt
    u_off = q_off + 3 * diff_w
    n_main = u_off + 2 * gmlp_w
    col_scale = jnp.ones((1, n_main), F32).at[:, q_off:q_off + diff_w].set(DIFF_HEAD_DIM ** -0.5 * LOG2E)

    x = [x_prompt.reshape(n_prompt, d), x_sample.reshape(sb * ss, d)]
    h = _rmsnorm(x, norms[0, 0])
    for l in range(depth):
        n = norms[l]
        lambda_init = 0.8 - 0.6 * math.exp(-0.3 * l)
        wi = w_in[l]
        w_main = jnp.concatenate([wi[:, :c_dt], wi[:, c_diff:c_gate]], axis=1).astype(BF16)
        w_dt = jnp.pad(wi[:, c_dt:c_diff], ((0, 0), (0, LANES - 2 * heads))).astype(BF16)
        w_gate = wi[:, c_gate:].astype(BF16)

        y = _matmul(_matmul_swiglu(h, w_ffn1_in[l].astype(BF16)), w_ffn1_out[l].astype(BF16), BF16, tn=FFN_OUT_TN)
        x, h = _resid_norm(x, y, n[1], 0.5, n[2])
        x = [x]

        proj = _matmul(h, w_main, BF16, scale=col_scale)
        dt_raw = _matmul(h, w_dt, F32, tn=LANES)
        xbc = _conv_silu(proj, d_inner, conv_w[l], conv_b[l], seq_lens)
        y_ssm = _ssd(xbc, proj, dt_raw, dt_bias[l], a_log[l], d_skip[l], ssm_norm[l], seq_lens, groups)
        y_att = [
            _diff_attention(proj, q_off, diff_heads, rel_bias, lambda_qk[l], diff_subln[l], lambda_init, 0, pb, ps),
            _diff_attention(proj, q_off, diff_heads, rel_bias, lambda_qk[l], diff_subln[l], lambda_init,
                            n_prompt, sb, ss)]
        y_gmlp = _gmlp(proj, u_off, gmlp_w, ln_v[l], w_spatial[l], b_spatial[l])
        merged = _merge(h, y_ssm, y_att, y_gmlp, w_gate, w_o_ssm[l].astype(BF16), w_o_diff[l].astype(BF16),
                        w_o_gmlp[l].astype(BF16))
        y = _matmul(merged, w_out[l].astype(BF16), BF16)
        x, h = _resid_norm(x, y, n[3], 1.0, n[4])
        x = [x]

        y = _matmul(_matmul_swiglu(h, w_ffn2_in[l].astype(BF16)), w_ffn2_out[l].astype(BF16), BF16, tn=FFN_OUT_TN)
        if l + 1 < depth:
            x, h = _resid_norm(x, y, n[5], 0.5, norms[l + 1, 0])
            x = [x]

    y_prompt = _resid_rows(x[0], y, norms[depth - 1, 5], 0.5, 0, n_prompt)
    y_sample = _resid_rows(x[0], y, norms[depth - 1, 5], 0.5, n_prompt, sb * ss)
    return y_prompt.reshape(x_prompt.shape), y_sample.reshape(x_sample.shape)
```

```python
import functools
import math

import jax
import jax.numpy as jnp
from jax import lax
from jax.experimental import pallas as pl
from jax.experimental.pallas import tpu as pltpu

F32 = jnp.float32
BF16 = jnp.bfloat16

SSM_HEAD_DIM = 64
SSM_STATE = 128
SSD_CHUNK = 128
DIFF_HEAD_DIM = 128
NUM_BUCKETS = 32
MAX_DISTANCE = 128
GMLP_CHUNK = 128
NORM_EPS = 1e-6
SUBLN_EPS = 1e-5
LN_EPS = 1e-5
LOG2E = math.log2(math.e)

LANES = 128
BF16_SUBLANES = 16
VMEM_LIMIT_BYTES = 56 * 1024 * 1024

MM_TM = 1024
MM_TN = 1024
NORM_MM_TM = 512
FFN_OUT_TN = 2048
ROW_TILE = 256
CONV_ROWS = 1024
CONV_COLS = 1024
CONV_BLOCK = 128
ATTN_TILE = 512
FAR_W = 4
MERGE_TM = 512
MERGE_TN = 512
MERGE_VMEM_LIMIT_BYTES = 58 * 1024 * 1024
GMLP_ROWS = 512


def _cparams(sem, vmem_limit_bytes=VMEM_LIMIT_BYTES):
    return pltpu.CompilerParams(dimension_semantics=sem, vmem_limit_bytes=vmem_limit_bytes)


def _dot(a, b):
    return jnp.dot(a, b, preferred_element_type=F32)


def _dot_nt(a, b):
    return lax.dot_general(a, b, (((1,), (1,)), ((), ())), preferred_element_type=F32)


def _dot_tn(a, b):
    return lax.dot_general(a, b, (((0,), (0,)), ((), ())), preferred_element_type=F32)


def _split_bf16(x, n):
    parts = []
    r = x
    for _ in range(n):
        p = r.astype(BF16)
        parts.append(p)
        r = r - p.astype(F32)
    return parts


def _rms(x, eps):
    return x * lax.rsqrt(jnp.mean(x * x, axis=-1, keepdims=True) + eps)


def _silu(x):
    return x * jax.nn.sigmoid(x)


def _softplus(x):
    return jnp.maximum(x, 0.0) + jnp.log1p(jnp.exp(-jnp.abs(x)))


def _mm_scale_kernel(a_ref, w_ref, s_ref, o_ref):
    o_ref[...] = (_dot(a_ref[...], w_ref[...]) * s_ref[...]).astype(o_ref.dtype)


def _mm_kernel(a_ref, w_ref, o_ref):
    o_ref[...] = _dot(a_ref[...], w_ref[...]).astype(o_ref.dtype)


def _mm_swiglu_kernel(a_ref, wg_ref, wu_ref, o_ref):
    a = a_ref[...]
    g = _dot(a, wg_ref[...])
    u = _dot(a, wu_ref[...])
    o_ref[...] = (_silu(g) * u).astype(o_ref.dtype)


def _matmul(a, w, out_dtype, scale=None, tm=None, tn=None):
    m, k = a.shape
    n = w.shape[1]
    tm = min(tm or MM_TM, m)
    tn = min(tn or MM_TN, n)
    in_specs = [pl.BlockSpec((tm, k), lambda i, j: (i, 0)), pl.BlockSpec((k, tn), lambda i, j: (0, j))]
    args = [a, w]
    body = _mm_kernel
    if scale is not None:
        in_specs.append(pl.BlockSpec((1, tn), lambda i, j: (0, j)))
        args.append(scale)
        body = _mm_scale_kernel
    return pl.pallas_call(
        body, name="mm", grid=(m // tm, n // tn), in_specs=in_specs,
        out_specs=pl.BlockSpec((tm, tn), lambda i, j: (i, j)),
        out_shape=jax.ShapeDtypeStruct((m, n), out_dtype),
        compiler_params=_cparams(("parallel", "arbitrary")))(*args)


def _norm_swiglu_kernel(x_ref, g_ref, wg_ref, wu_ref, o_ref, h_ref):
    @pl.when(pl.program_id(1) == 0)
    def _():
        h_ref[...] = (_rms(x_ref[...], NORM_EPS) * g_ref[...]).astype(h_ref.dtype)

    a = h_ref[...]
    o_ref[...] = (_silu(_dot(a, wg_ref[...])) * _dot(a, wu_ref[...])).astype(o_ref.dtype)


def _norm_matmul_swiglu(x, g, w):
    m, k = x.shape
    n = w.shape[1] // 2
    tm = min(NORM_MM_TM, m)
    tn = min(MM_TN // 2, n)
    nj = n // tn
    return pl.pallas_call(
        _norm_swiglu_kernel, name="norm_mm_swiglu", grid=(m // tm, nj),
        in_specs=[pl.BlockSpec((tm, k), lambda i, j: (i, 0)),
                  pl.BlockSpec((1, k), lambda i, j: (0, 0)),
                  pl.BlockSpec((k, tn), lambda i, j: (0, j)),
                  pl.BlockSpec((k, tn), lambda i, j: (0, j + nj))],
        out_specs=pl.BlockSpec((tm, tn), lambda i, j: (i, j)),
        out_shape=jax.ShapeDtypeStruct((m, n), BF16),
        scratch_shapes=[pltpu.VMEM((tm, k), BF16)],
        compiler_params=_cparams(("parallel", "arbitrary")))(x, g.reshape(1, k), w, w)


def _matmul_swiglu(a, w):
    m, k = a.shape
    n = w.shape[1] // 2
    tm = min(MM_TM, m)
    tn = min(MM_TN // 2, n)
    nj = n // tn
    return pl.pallas_call(
        _mm_swiglu_kernel, name="mm_swiglu", grid=(m // tm, nj),
        in_specs=[pl.BlockSpec((tm, k), lambda i, j: (i, 0)),
                  pl.BlockSpec((k, tn), lambda i, j: (0, j)),
                  pl.BlockSpec((k, tn), lambda i, j: (0, j + nj))],
        out_specs=pl.BlockSpec((tm, tn), lambda i, j: (i, j)),
        out_shape=jax.ShapeDtypeStruct((m, n), BF16),
        compiler_params=_cparams(("parallel", "arbitrary")))(a, w, w)


def _stream_specs(parts, tr):
    d = parts[0].shape[1]
    if len(parts) == 1:
        return [pl.BlockSpec((tr, d), lambda i: (i, 0))], None
    na = parts[0].shape[0] // tr
    return [pl.BlockSpec((tr, d), lambda i: (jnp.minimum(i, na - 1), 0)),
            pl.BlockSpec((tr, d), lambda i: (jnp.maximum(i - na, 0), 0))], na


def _stream_block(x_refs, first_blocks):
    if len(x_refs) == 1:
        return x_refs[0][...]
    return jnp.where(pl.program_id(0) < first_blocks, x_refs[0][...], x_refs[1][...])


def _resid_norm_kernel(*refs, n_parts, first_blocks, scale):
    y_ref, g1_ref, g2_ref, xo_ref, h_ref = refs[n_parts:]
    xn = _stream_block(refs[:n_parts], first_blocks) + scale * (_rms(y_ref[...].astype(F32), NORM_EPS) * g1_ref[...])
    xo_ref[...] = xn
    h_ref[...] = (_rms(xn, NORM_EPS) * g2_ref[...]).astype(h_ref.dtype)


def _resid_kernel(x_ref, y_ref, g1_ref, xo_ref, *, scale):
    xo_ref[...] = x_ref[...] + scale * (_rms(y_ref[...].astype(F32), NORM_EPS) * g1_ref[...])


def _resid_norm(parts, y, g_post, scale, g_next):
    t, d = y.shape
    tr = min(ROW_TILE, min(p.shape[0] for p in parts))
    x_specs, first_blocks = _stream_specs(parts, tr)
    row = pl.BlockSpec((tr, d), lambda i: (i, 0))
    vec = pl.BlockSpec((1, d), lambda i: (0, 0))
    return pl.pallas_call(
        functools.partial(_resid_norm_kernel, n_parts=len(parts), first_blocks=first_blocks, scale=scale),
        name="resid_norm", grid=(t // tr,),
        in_specs=x_specs + [row, vec, vec], out_specs=[row, row],
        out_shape=[jax.ShapeDtypeStruct((t, d), F32), jax.ShapeDtypeStruct((t, d), BF16)],
        compiler_params=_cparams(("parallel",)))(*parts, y, g_post.reshape(1, d), g_next.reshape(1, d))


def _resid_rows(x, y, g_post, scale, row_off, n_rows):
    d = x.shape[1]
    tr = min(ROW_TILE, n_rows)
    assert row_off % tr == 0 and n_rows % tr == 0
    b0 = row_off // tr
    row = pl.BlockSpec((tr, d), lambda i: (b0 + i, 0))
    return pl.pallas_call(
        functools.partial(_resid_kernel, scale=scale), name="resid", grid=(n_rows // tr,),
        in_specs=[row, row, pl.BlockSpec((1, d), lambda i: (0, 0))],
        out_specs=pl.BlockSpec((tr, d), lambda i: (i, 0)),
        out_shape=jax.ShapeDtypeStruct((n_rows, d), F32),
        compiler_params=_cparams(("parallel",)))(x, y, g_post.reshape(1, d))


def _conv_kernel(prev_ref, x_ref, next_ref, w_ref, b_ref, sh_ref, o_ref, *, rows, width, start_tiles, end_tiles):
    i = pl.program_id(0)
    halo = BF16_SUBLANES
    pad = width // 2
    blk = CONV_BLOCK
    is_start = functools.reduce(jnp.logical_or, [i == s for s in start_tiles])
    is_end = functools.reduce(jnp.logical_or, [i == e for e in end_tiles])
    prev = prev_ref[...]
    nxt = next_ref[...]
    ext = jnp.concatenate([jnp.where(is_start, jnp.zeros_like(prev), prev), x_ref[...],
                           jnp.where(is_end, jnp.zeros_like(nxt), nxt)], axis=0)
    w = w_ref[...]
    bias = b_ref[...]
    select = sh_ref[...]
    taps = [k for k in range(width) if k != pad]
    for rb in range(rows // blk):
        shifted = _dot(select, ext[rb * blk:rb * blk + blk + 2 * halo, :])
        acc = bias + w[pad:pad + 1, :] * x_ref[rb * blk:(rb + 1) * blk, :].astype(F32)
        for n, k in enumerate(taps):
            acc = acc + w[k:k + 1, :] * shifted[n * blk:(n + 1) * blk, :]
        o_ref[rb * blk:(rb + 1) * blk, :] = _silu(acc).astype(o_ref.dtype)


def _conv_silu(proj, col_off, conv_w, conv_b, seq_lens):
    t = proj.shape[0]
    width, ch = conv_w.shape
    rows = min(CONV_ROWS, min(seq_lens))
    cols = min(CONV_COLS, ch)
    halo = BF16_SUBLANES
    blk = CONV_BLOCK
    pad = width // 2
    assert col_off % cols == 0 and ch % cols == 0 and all(s % rows == 0 for s in seq_lens)
    assert rows % blk == 0 and pad <= halo
    starts, ends, pos = [], [], 0
    for s in seq_lens:
        starts.append(pos // rows)
        pos += s
        ends.append(pos // rows - 1)
    rb = rows // halo
    nhalo = t // halo
    cb = col_off // cols
    taps = jnp.array([k for k in range(width) if k != pad])
    src = jnp.arange(blk)[None, :] + halo + (taps - pad)[:, None]
    select = (src.reshape(-1)[:, None] == jnp.arange(blk + 2 * halo)[None, :]).astype(BF16)
    return pl.pallas_call(
        functools.partial(_conv_kernel, rows=rows, width=width, start_tiles=tuple(starts), end_tiles=tuple(ends)),
        name="conv_silu", grid=(t // rows, ch // cols),
        in_specs=[pl.BlockSpec((halo, cols), lambda i, j: (jnp.maximum(i * rb - 1, 0), cb + j)),
                  pl.BlockSpec((rows, cols), lambda i, j: (i, cb + j)),
                  pl.BlockSpec((halo, cols), lambda i, j: (jnp.minimum((i + 1) * rb, nhalo - 1), cb + j)),
                  pl.BlockSpec((width, cols), lambda i, j: (0, j)),
                  pl.BlockSpec((1, cols), lambda i, j: (0, j)),
                  pl.BlockSpec(select.shape, lambda i, j: (0, 0))],
        out_specs=pl.BlockSpec((rows, cols), lambda i, j: (i, j)),
        out_shape=jax.ShapeDtypeStruct((t, ch), BF16),
        compiler_params=_cparams(("parallel", "arbitrary")))(
            proj, proj, proj, conv_w, conv_b.reshape(1, ch), select)


def _ssd_kernel(*refs, reverse, n_chunks, reset_chunks, groups, heads, d_inner):
    if reverse:
        (xbc_ref, dt_ref, dtb_ref, alog_ref, tri_ref, trit_ref, exp_ref,
         yf_ref, z_ref, nw_ref, o_ref, state_ref) = refs
    else:
        (xbc_ref, dt_ref, dtb_ref, alog_ref, tri_ref, trit_ref, exp_ref,
         dskip_ref, o_ref, state_ref) = refs
    step = pl.program_id(0)
    chunk = (n_chunks - 1 - step) if reverse else step
    L = SSD_CHUNK
    gw = d_inner // groups
    hpg = heads // groups
    n = SSM_STATE
    col0 = heads if reverse else 0

    @pl.when(functools.reduce(jnp.logical_or, [chunk == c for c in reset_chunks]))
    def _():
        state_ref[...] = jnp.zeros_like(state_ref)

    dt = _softplus(dt_ref[...] + dtb_ref[...])
    dta = dt * (-jnp.exp(alog_ref[...]))
    dt_t = dt.T
    dta_t = dta.T
    tri = tri_ref[...]
    tri_t = trit_ref[...]
    acum = sum(_dot(tri, p) for p in _split_bf16(dta, 3))
    acum_t = sum(_dot(p, tri_t) for p in _split_bf16(dta_t, 3))
    edge = 0 if reverse else L - 1
    a_end = acum[edge:edge + 1, :]
    to_end = jnp.exp(a_end - acum) * dt
    ea = jnp.exp(acum)
    expand = exp_ref[...]
    to_end_x = sum(_dot(p, expand) for p in _split_bf16(to_end, 2))
    ea_x = sum(_dot(p, expand) for p in _split_bf16(ea, 2))
    decay_x = ea_x[edge:edge + 1, :]
    acum2 = acum * LOG2E
    row_t = acum_t * LOG2E - jnp.log2(dt_t)

    row = lax.broadcasted_iota(jnp.int32, (L, L), 0)
    colm = lax.broadcasted_iota(jnp.int32, (L, L), 1)
    causal = (row <= colm) if reverse else (row >= colm)
    lane = lax.broadcasted_iota(jnp.int32, (L, LANES), 1)
    first_half = lane < SSM_HEAD_DIM

    for g in range(groups):
        xs = xbc_ref[:, g * gw:(g + 1) * gw]
        bm = xbc_ref[:, d_inner + g * n:d_inner + (g + 1) * n]
        cm = xbc_ref[:, d_inner + groups * n + g * n:d_inner + groups * n + (g + 1) * n]
        cb = _dot_nt(cm, bm)
        st = state_ref[g]
        y_off = _dot(cm, st.astype(BF16)) * ea_x[:, g * gw:(g + 1) * gw]
        xs_scaled = (xs.astype(F32) * to_end_x[:, g * gw:(g + 1) * gw]).astype(BF16)
        state_ref[g] = st * decay_x[:, g * gw:(g + 1) * gw] + _dot_tn(bm, xs_scaled)
        pairs = []
        for k in range(hpg // 2):
            xp = xs[:, k * LANES:(k + 1) * LANES]
            y_pair = None
            for side in range(2):
                c = col0 + g * hpg + 2 * k + side
                seg = acum2[:, c:c + 1] - row_t[c:c + 1, :]
                m = (cb * jnp.exp2(jnp.where(causal, seg, -jnp.inf))).astype(BF16)
                keep = first_half if side == 0 else jnp.logical_not(first_half)
                part = _dot(m, jnp.where(keep, xp, jnp.zeros_like(xp)))
                y_pair = part if y_pair is None else y_pair + part
            lo = g * gw + k * LANES
            y_pair = y_pair + y_off[:, k * LANES:(k + 1) * LANES]
            if not reverse:
                o_ref[:, lo:lo + LANES] = y_pair + dskip_ref[:, lo:lo + LANES] * xp.astype(F32)
            else:
                pairs.append(y_pair)
        if reverse:
            y = jnp.concatenate(pairs, axis=1) + yf_ref[:, g * gw:(g + 1) * gw]
            y = y * _silu(z_ref[:, g * gw:(g + 1) * gw].astype(F32))
            o_ref[:, g * gw:(g + 1) * gw] = (_rms(y, NORM_EPS) * nw_ref[:, g * gw:(g + 1) * gw]).astype(o_ref.dtype)


def _ssd(xbc, proj, dt_raw, dt_bias, a_log, d_skip, norm_w, seq_lens, groups):
    t = xbc.shape[0]
    heads = d_skip.shape[0]
    d_inner = heads * SSM_HEAD_DIM
    L = SSD_CHUNK
    nc = t // L
    assert 2 * heads <= LANES and (heads // groups) % 2 == 0
    starts, lasts, pos = [], [], 0
    for s in seq_lens:
        assert s % L == 0
        starts.append(pos // L)
        pos += s
        lasts.append(pos // L - 1)
    padh = LANES - 2 * heads
    dtb = jnp.pad(dt_bias.reshape(-1), (0, padh))
    alog = jnp.pad(a_log.reshape(-1), (0, padh))
    tri_lo = jnp.tril(jnp.ones((L, L), F32)).astype(BF16)
    tri_up = jnp.triu(jnp.ones((L, L), F32)).astype(BF16)
    head_of_col = jnp.arange(d_inner) // SSM_HEAD_DIM
    dskip_x = jnp.repeat(d_skip.astype(F32), SSM_HEAD_DIM).reshape(1, d_inner)

    def run(reverse, extra_args, extra_specs, out_dtype):
        cidx = (lambda i: nc - 1 - i) if reverse else (lambda i: i)
        tri = tri_up if reverse else tri_lo
        expand = (jnp.arange(LANES)[:, None] == (head_of_col + (heads if reverse else 0))[None, :]).astype(BF16)
        const = lambda shape: pl.BlockSpec(shape, lambda i: (0, 0))
        in_specs = [pl.BlockSpec((L, xbc.shape[1]), lambda i: (cidx(i), 0)),
                    pl.BlockSpec((L, LANES), lambda i: (cidx(i), 0)),
                    const((1, LANES)), const((1, LANES)),
                    const((L, L)), const((L, L)), const((LANES, d_inner))] + extra_specs(cidx)
        return pl.pallas_call(
            functools.partial(_ssd_kernel, reverse=reverse, n_chunks=nc,
                              reset_chunks=tuple(lasts if reverse else starts),
                              groups=groups, heads=heads, d_inner=d_inner),
            name="ssd_bwd" if reverse else "ssd_fwd", grid=(nc,), in_specs=in_specs,
            out_specs=pl.BlockSpec((L, d_inner), lambda i: (cidx(i), 0)),
            out_shape=jax.ShapeDtypeStruct((t, d_inner), out_dtype),
            scratch_shapes=[pltpu.VMEM((groups, SSM_STATE, d_inner // groups), F32)],
            compiler_params=_cparams(("arbitrary",)))(
                xbc, dt_raw, dtb.reshape(1, LANES), alog.reshape(1, LANES), tri, tri.T, expand, *extra_args)

    y_fwd = run(False, [dskip_x], lambda cidx: [pl.BlockSpec((1, d_inner), lambda i: (0, 0))], F32)
    return run(True, [y_fwd, proj, norm_w.reshape(1, d_inner)],
               lambda cidx: [pl.BlockSpec((L, d_inner), lambda i: (cidx(i), 0)),
                             pl.BlockSpec((L, d_inner), lambda i: (cidx(i), 0)),
                             pl.BlockSpec((1, d_inner), lambda i: (0, 0))], BF16)


def _t5_bucket(rel):
    half = NUM_BUCKETS // 2
    max_exact = half // 2
    n = jnp.abs(rel)
    log_ratio = jnp.log(jnp.maximum(n, 1).astype(F32) / max_exact) / math.log(MAX_DISTANCE / max_exact)
    large = jnp.minimum(max_exact + (log_ratio * (half - max_exact)).astype(jnp.int32), half - 1)
    return jnp.where(rel > 0, half, 0) + jnp.where(n < max_exact, n, large)


def _attn_kernel(far_ref, lq_ref, sw_ref, q_ref, k_ref, v_ref, bias_ref, o_ref, m_ref, l_ref, acc_ref,
                 sa_ref, sb_ref, *, t, nk, lambda_init):
    h = pl.program_id(1)
    i = pl.program_id(2)
    dh = DIFF_HEAD_DIM
    nblk = t // LANES
    c_before = far_ref[h, 0]
    c_after = far_ref[h, 1]
    m_ref[...] = jnp.full(m_ref.shape, -jnp.inf, F32)
    l_ref[...] = jnp.zeros(l_ref.shape, F32)
    acc_ref[...] = jnp.zeros(acc_ref.shape, F32)

    def qk(kb, j, width=1):
        ks = k_ref[pl.ds(pl.multiple_of(kb * t, t), width * t), j * dh:(j + 1) * dh]
        return _dot_nt(q_ref[:, j * dh:(j + 1) * dh], ks)

    def softmax_pv(kb, s_of, shift, width=1):
        vs = v_ref[pl.ds(pl.multiple_of(kb * t, t), width * t), :]
        for j in range(2):
            s = s_of(j)
            blocks = [s[:, c * LANES:(c + 1) * LANES] for c in range(width * nblk)]
            row_max = jnp.max(functools.reduce(jnp.maximum, blocks), axis=-1, keepdims=True)
            m_old = m_ref[j]
            m_new = jnp.maximum(m_old, row_max + shift)
            alpha = jnp.exp2(m_old - m_new)
            ref = m_new - shift
            ps = [jnp.exp2(b - ref) for b in blocks]
            l_ref[j] = alpha * l_ref[j] + functools.reduce(jnp.add, ps)
            p = jnp.concatenate([x.astype(BF16) for x in ps], axis=1)
            acc_ref[j] = jnp.concatenate([alpha] * (acc_ref.shape[2] // LANES), axis=1) * acc_ref[j] + _dot(p, vs)
            m_ref[j] = m_new

    def update(kb, bias_tile):
        softmax_pv(kb, lambda j: qk(kb, j) + bias_tile, 0.0)

    def far_loop(lo, hi, shift):
        w = FAR_W
        n = hi - lo
        nd = n // w
        base = lo + n % w

        def single(kb, carry):
            softmax_pv(kb, lambda j: qk(kb, j), shift)
            return carry

        lax.fori_loop(lo, base, single, 0)

        @pl.when(nd > 0)
        def _():
            for j in range(2):
                sa_ref[j] = qk(base, j, w)

            def pair(kp, carry):
                d0 = base + 2 * w * kp
                d1 = jnp.minimum(d0 + w, hi - w)
                d2 = jnp.minimum(d0 + 2 * w, hi - w)
                for j in range(2):
                    sb_ref[j] = qk(d1, j, w)
                softmax_pv(d0, lambda j: sa_ref[j], shift, w)
                for j in range(2):
                    sa_ref[j] = qk(d2, j, w)
                softmax_pv(d1, lambda j: sb_ref[j], shift, w)
                return carry

            lax.fori_loop(0, nd // 2, pair, 0)

            @pl.when(nd % 2 == 1)
            def _():
                softmax_pv(hi - w, lambda j: sa_ref[j], shift, w)

    far_loop(0, jnp.maximum(i - 1, 0), c_before)

    @pl.when(i >= 1)
    def _():
        update(i - 1, bias_ref[0, 0])

    update(i, bias_ref[0, 1])

    @pl.when(i + 1 < nk)
    def _():
        update(i + 1, bias_ref[0, 2])

    far_loop(jnp.minimum(i + 2, nk), nk, c_after)

    lq = lq_ref[...]
    lam = (jnp.exp(jnp.sum(lq[0:1] * lq[1:2], axis=-1, keepdims=True))
           - jnp.exp(jnp.sum(lq[2:3] * lq[3:4], axis=-1, keepdims=True)) + lambda_init)
    l0 = jnp.sum(l_ref[0], axis=-1, keepdims=True)
    l1 = jnp.sum(l_ref[1], axis=-1, keepdims=True)
    o = acc_ref[0] / l0 - lam * (acc_ref[1] / l1)
    o_ref[...] = (_rms(o, SUBLN_EPS) * sw_ref[...] * (1.0 - lambda_init)).astype(o_ref.dtype)


def _near_bias_tiles(table_t, t):
    x = jnp.concatenate([jnp.arange(t), jnp.arange(t) - t])
    rel = (jnp.arange(-1, 2) * t)[:, None] + x[None, :]
    w = table_t[:, _t5_bucket(rel)]
    flat = jnp.tile(w, (1, 1, t))[..., :t * (2 * t - 1)]
    return flat.reshape(w.shape[0], 3, t, 2 * t - 1)[..., :t]


def _diff_attention(proj, q_off, n_heads, rel_bias, lambda_qk, subln_w, lambda_init, tok_off, bsz, seq):
    t = min(ATTN_TILE, seq)
    nk = seq // t
    hw = 2 * DIFF_HEAD_DIM
    assert t >= MAX_DISTANCE and seq % t == 0 and tok_off % seq == 0 and q_off % hw == 0
    qb = q_off // hw
    table_t = rel_bias.astype(F32).T * LOG2E
    bias_near = _near_bias_tiles(table_t, t)
    far = table_t[:, _t5_bucket(jnp.array([-2 * MAX_DISTANCE, 2 * MAX_DISTANCE]))]
    row0 = tok_off // t
    seq0 = tok_off // seq
    return pl.pallas_call(
        functools.partial(_attn_kernel, t=t, nk=nk, lambda_init=lambda_init),
        name="diff_attn", grid=(bsz, n_heads, nk),
        in_specs=[pl.BlockSpec(memory_space=pltpu.SMEM),
                  pl.BlockSpec((4, DIFF_HEAD_DIM), lambda b, h, i: (0, 0)),
                  pl.BlockSpec((1, hw), lambda b, h, i: (0, 0)),
                  pl.BlockSpec((t, hw), lambda b, h, i: (row0 + b * nk + i, qb + h)),
                  pl.BlockSpec((seq, hw), lambda b, h, i: (seq0 + b, qb + n_heads + h),
                               pipeline_mode=pl.Buffered(1)),
                  pl.BlockSpec((seq, hw), lambda b, h, i: (seq0 + b, qb + 2 * n_heads + h),
                               pipeline_mode=pl.Buffered(1)),
                  pl.BlockSpec((1, 3, t, t), lambda b, h, i: (h, 0, 0, 0))],
        out_specs=pl.BlockSpec((t, hw), lambda b, h, i: (b * nk + i, h)),
        out_shape=jax.ShapeDtypeStruct((bsz * seq, n_heads * hw), BF16),
        scratch_shapes=[pltpu.VMEM((2, t, LANES), F32), pltpu.VMEM((2, t, LANES), F32),
                        pltpu.VMEM((2, t, hw), F32), pltpu.VMEM((2, t, FAR_W * t), F32),
                        pltpu.VMEM((2, t, FAR_W * t), F32)],
        compiler_params=_cparams(("parallel", "parallel", "arbitrary")))(
            far, lambda_qk.astype(F32), subln_w.reshape(1, hw).astype(F32), proj, proj, proj, bias_near)


def _gelu_tanh(x):
    return 0.5 * x * (1.0 + jnp.tanh(math.sqrt(2.0 / math.pi) * (x + 0.044715 * (x * x * x))))


def _gmlp_kernel(u_ref, v_ref, lng_ref, lnb_ref, ws_ref, bs_ref, o_ref, *, rows, groups):
    u = _gelu_tanh(u_ref[...].astype(F32))
    v = _gelu_tanh(v_ref[...].astype(F32))
    mu = jnp.mean(v, axis=-1, keepdims=True)
    vc = v - mu
    vn = (vc * lax.rsqrt(jnp.mean(vc * vc, axis=-1, keepdims=True) + LN_EPS) * lng_ref[...] + lnb_ref[...]).astype(BF16)
    gd = v.shape[1] // groups
    L = GMLP_CHUNK
    for c in range(rows // L):
        for g in range(groups):
            mixed = _dot(ws_ref[g], vn[c * L:(c + 1) * L, g * gd:(g + 1) * gd]) + bs_ref[:, g * gd:(g + 1) * gd]
            o_ref[c * L:(c + 1) * L, g * gd:(g + 1) * gd] = (
                u[c * L:(c + 1) * L, g * gd:(g + 1) * gd] * mixed).astype(o_ref.dtype)


def _gmlp(proj, u_off, width, ln_v, w_s, b_s):
    t = proj.shape[0]
    groups = w_s.shape[0]
    L = GMLP_CHUNK
    rows = min(GMLP_ROWS, t)
    assert u_off % width == 0 and rows % L == 0
    ub = u_off // width
    bias_x = jnp.repeat(b_s.astype(F32).T, width // groups, axis=1)
    vec = pl.BlockSpec((1, width), lambda i: (0, 0))
    return pl.pallas_call(
        functools.partial(_gmlp_kernel, rows=rows, groups=groups), name="gmlp", grid=(t // rows,),
        in_specs=[pl.BlockSpec((rows, width), lambda i: (i, ub)),
                  pl.BlockSpec((rows, width), lambda i: (i, ub + 1)),
                  vec, vec,
                  pl.BlockSpec((groups, L, L), lambda i: (0, 0, 0)),
                  pl.BlockSpec((L, width), lambda i: (0, 0))],
        out_specs=pl.BlockSpec((rows, width), lambda i: (i, 0)),
        out_shape=jax.ShapeDtypeStruct((t, width), BF16),
        compiler_params=_cparams(("parallel",)))(
            proj, proj, ln_v[0].reshape(1, width).astype(F32), ln_v[1].reshape(1, width).astype(F32),
            w_s.astype(BF16), bias_x)


def _merge_kernel(h_ref, s_ref, a0_ref, a1_ref, m_ref, wg0_ref, wg1_ref, wg2_ref, wos_ref, woa_ref, wom_ref,
                  o_ref, *, first_blocks):
    def gated_sum(a_ref):
        h = h_ref[...]
        merged = (jax.nn.sigmoid(_dot(h, wg0_ref[...])) * _dot(s_ref[...], wos_ref[...])
                  + jax.nn.sigmoid(_dot(h, wg1_ref[...])) * _dot(a_ref[...], woa_ref[...])
                  + jax.nn.sigmoid(_dot(h, wg2_ref[...])) * _dot(m_ref[...], wom_ref[...]))
        o_ref[...] = merged.astype(o_ref.dtype)

    in_first = pl.program_id(0) < first_blocks
    pl.when(in_first)(lambda: gated_sum(a0_ref))
    pl.when(jnp.logical_not(in_first))(lambda: gated_sum(a1_ref))


def _merge(h, y_s, y_a_parts, y_m, w_gate, w_os, w_oa, w_om):
    t, d = h.shape
    ya0, ya1 = y_a_parts
    tm = min(MERGE_TM, ya0.shape[0], ya1.shape[0])
    tn = min(MERGE_TN, d)
    nj = d // tn
    na = ya0.shape[0] // tm
    assert ya0.shape[0] % tm == 0 and ya1.shape[0] % tm == 0
    act = lambda a: pl.BlockSpec((tm, a.shape[1]), lambda i, j: (i, 0))
    wcol = lambda w, off: pl.BlockSpec((w.shape[0], tn), lambda i, j: (0, j + off))
    return pl.pallas_call(
        functools.partial(_merge_kernel, first_blocks=na), name="merge", grid=(t // tm, nj),
        in_specs=[act(h), act(y_s),
                  pl.BlockSpec((tm, ya0.shape[1]), lambda i, j: (jnp.minimum(i, na - 1), 0)),
                  pl.BlockSpec((tm, ya1.shape[1]), lambda i, j: (jnp.maximum(i - na, 0), 0)),
                  act(y_m),
                  wcol(w_gate, 0), wcol(w_gate, nj), wcol(w_gate, 2 * nj),
                  wcol(w_os, 0), wcol(w_oa, 0), wcol(w_om, 0)],
        out_specs=pl.BlockSpec((tm, tn), lambda i, j: (i, j)),
        out_shape=jax.ShapeDtypeStruct((t, d), BF16),
        compiler_params=_cparams(("parallel", "arbitrary"), MERGE_VMEM_LIMIT_BYTES))(
            h, y_s, ya0, ya1, y_m, w_gate, w_gate, w_gate, w_os, w_oa, w_om)


def kernel(x_prompt, x_sample, rel_bias, norms, w_ffn1_in, w_ffn1_out, w_in, conv_w, conv_b, dt_bias, a_log,
           d_skip, ssm_norm, w_o_ssm, lambda_qk, diff_subln, w_o_diff, ln_v, w_spatial, b_spatial, w_o_gmlp,
           w_out, w_ffn2_in, w_ffn2_out):
    depth = norms.shape[0]
    d = x_prompt.shape[-1]
    d_inner = w_o_ssm.shape[1]
    conv_ch = conv_w.shape[2]
    heads = d_skip.shape[1]
    groups = (conv_ch - d_inner) // (2 * SSM_STATE)
    diff_w = w_o_diff.shape[1]
    diff_heads = diff_w // (2 * DIFF_HEAD_DIM)
    gmlp_w = w_o_gmlp.shape[1]
    pb, ps = x_prompt.shape[:2]
    sb, ss = x_sample.shape[:2]
    seq_lens = [ps] * pb + [ss] * sb
    n_prompt = pb * ps

    c_dt = d_inner + conv_ch
    c_diff = c_dt + 2 * heads
    c_gmlp = c_diff + 3 * diff_w
    c_gate = c_gmlp + 2 * gmlp_w
    q_off = c_dt
    u_off = q_off + 3 * diff_w
    n_main = u_off + 2 * gmlp_w
    col_scale = jnp.ones((1, n_main), F32).at[:, q_off:q_off + diff_w].set(DIFF_HEAD_DIM ** -0.5 * LOG2E)

    x = [x_prompt.reshape(n_prompt, d), x_sample.reshape(sb * ss, d)]
    for l in range(depth):
        n = norms[l]
        lambda_init = 0.8 - 0.6 * math.exp(-0.3 * l)
        wi = w_in[l]
        w_main = jnp.concatenate([wi[:, :c_dt], wi[:, c_diff:c_gate]], axis=1).astype(BF16)
        w_dt = jnp.pad(wi[:, c_dt:c_diff], ((0, 0), (0, LANES - 2 * heads))).astype(BF16)
        w_gate = wi[:, c_gate:].astype(BF16)

        if l == 0:
            hidden = jnp.concatenate([_norm_matmul_swiglu(part, n[0], w_ffn1_in[l].astype(BF16)) for part in x])
        else:
            hidden = _matmul_swiglu(h, w_ffn1_in[l].astype(BF16))
        y = _matmul(hidden, w_ffn1_out[l].astype(BF16), BF16, tn=FFN_OUT_TN)
        x, h = _resid_norm(x, y, n[1], 0.5, n[2])
        x = [x]

        proj = _matmul(h, w_main, BF16, scale=col_scale)
        dt_raw = _matmul(h, w_dt, F32, tn=LANES)
        xbc = _conv_silu(proj, d_inner, conv_w[l], conv_b[l], seq_lens)
        y_ssm = _ssd(xbc, proj, dt_raw, dt_bias[l], a_log[l], d_skip[l], ssm_norm[l], seq_lens, groups)
        y_att = [
            _diff_attention(proj, q_off, diff_heads, rel_bias, lambda_qk[l], diff_subln[l], lambda_init, 0, pb, ps),
            _diff_attention(proj, q_off, diff_heads, rel_bias, lambda_qk[l], diff_subln[l], lambda_init,
                            n_prompt, sb, ss)]
        y_gmlp = _gmlp(proj, u_off, gmlp_w, ln_v[l], w_spatial[l], b_spatial[l])
        merged = _merge(h, y_ssm, y_att, y_gmlp, w_gate, w_o_ssm[l].astype(BF16), w_o_diff[l].astype(BF16),
                        w_o_gmlp[l].astype(BF16))
        y = _matmul(merged, w_out[l].astype(BF16), BF16)
        x, h = _resid_norm(x, y, n[3], 1.0, n[4])
        x = [x]

        y = _matmul(_matmul_swiglu(h, w_ffn2_in[l].astype(BF16)), w_ffn2_out[l].astype(BF16), BF16, tn=FFN_OUT_TN)
        if l + 1 < depth:
            x, h = _resid_norm(x, y, n[5], 0.5, norms[l + 1, 0])
            x = [x]

    y_prompt = _resid_rows(x[0], y, norms[depth - 1, 5], 0.5, 0, n_prompt)
    y_sample = _resid_rows(x[0], y, norms[depth - 1, 5], 0.5, n_prompt, sb * ss)
    return y_prompt.reshape(x_prompt.shape), y_sample.reshape(x_sample.shape)
```

```python
import functools
import math

import jax
import jax.numpy as jnp
from jax import lax
from jax.experimental import pallas as pl
from jax.experimental.pallas import tpu as pltpu

F32 = jnp.float32
BF16 = jnp.bfloat16

SSM_HEAD_DIM = 64
SSM_STATE = 128
SSD_CHUNK = 128
DIFF_HEAD_DIM = 128
NUM_BUCKETS = 32
MAX_DISTANCE = 128
GMLP_CHUNK = 128
NORM_EPS = 1e-6
SUBLN_EPS = 1e-5
LN_EPS = 1e-5
LOG2E = math.log2(math.e)

LANES = 128
BF16_SUBLANES = 16
VMEM_LIMIT_BYTES = 56 * 1024 * 1024

MM_TM = 1024
MM_TN = 1024
FFN_OUT_TN = 2048
ROW_TILE = 256
CONV_ROWS = 1024
CONV_COLS = 1024
CONV_BLOCK = 128
ATTN_TILE = 512
FAR_W = 4
MERGE_TM = 512
MERGE_TN = 512
MERGE_VMEM_LIMIT_BYTES = 58 * 1024 * 1024
GMLP_ROWS = 512


def _cparams(sem, vmem_limit_bytes=VMEM_LIMIT_BYTES):
    return pltpu.CompilerParams(dimension_semantics=sem, vmem_limit_bytes=vmem_limit_bytes)


def _dot(a, b):
    return jnp.dot(a, b, preferred_element_type=F32)


def _dot_nt(a, b):
    return lax.dot_general(a, b, (((1,), (1,)), ((), ())), preferred_element_type=F32)


def _dot_tn(a, b):
    return lax.dot_general(a, b, (((0,), (0,)), ((), ())), preferred_element_type=F32)


def _split_bf16(x, n):
    parts = []
    r = x
    for _ in range(n):
        p = r.astype(BF16)
        parts.append(p)
        r = r - p.astype(F32)
    return parts


def _rms(x, eps):
    return x * lax.rsqrt(jnp.mean(x * x, axis=-1, keepdims=True) + eps)


def _silu(x):
    return x * jax.nn.sigmoid(x)


def _softplus(x):
    return jnp.maximum(x, 0.0) + jnp.log1p(jnp.exp(-jnp.abs(x)))


def _mm_scale_kernel(a_ref, w_ref, s_ref, o_ref):
    o_ref[...] = (_dot(a_ref[...], w_ref[...]) * s_ref[...]).astype(o_ref.dtype)


def _mm_kernel(a_ref, w_ref, o_ref):
    o_ref[...] = _dot(a_ref[...], w_ref[...]).astype(o_ref.dtype)


def _mm_swiglu_kernel(a_ref, wg_ref, wu_ref, o_ref):
    a = a_ref[...]
    g = _dot(a, wg_ref[...])
    u = _dot(a, wu_ref[...])
    o_ref[...] = (_silu(g) * u).astype(o_ref.dtype)


def _matmul(a, w, out_dtype, scale=None, tm=None, tn=None):
    m, k = a.shape
    n = w.shape[1]
    tm = min(tm or MM_TM, m)
    tn = min(tn or MM_TN, n)
    in_specs = [pl.BlockSpec((tm, k), lambda i, j: (i, 0)), pl.BlockSpec((k, tn), lambda i, j: (0, j))]
    args = [a, w]
    body = _mm_kernel
    if scale is not None:
        in_specs.append(pl.BlockSpec((1, tn), lambda i, j: (0, j)))
        args.append(scale)
        body = _mm_scale_kernel
    return pl.pallas_call(
        body, name="mm", grid=(m // tm, n // tn), in_specs=in_specs,
        out_specs=pl.BlockSpec((tm, tn), lambda i, j: (i, j)),
        out_shape=jax.ShapeDtypeStruct((m, n), out_dtype),
        compiler_params=_cparams(("parallel", "arbitrary")))(*args)


def _matmul_swiglu(a, w):
    m, k = a.shape
    n = w.shape[1] // 2
    tm = min(MM_TM, m)
    tn = min(MM_TN // 2, n)
    nj = n // tn
    return pl.pallas_call(
        _mm_swiglu_kernel, name="mm_swiglu", grid=(m // tm, nj),
        in_specs=[pl.BlockSpec((tm, k), lambda i, j: (i, 0)),
                  pl.BlockSpec((k, tn), lambda i, j: (0, j)),
                  pl.BlockSpec((k, tn), lambda i, j: (0, j + nj))],
        out_specs=pl.BlockSpec((tm, tn), lambda i, j: (i, j)),
        out_shape=jax.ShapeDtypeStruct((m, n), BF16),
        compiler_params=_cparams(("parallel", "arbitrary")))(a, w, w)


def _stream_specs(parts, tr):
    d = parts[0].shape[1]
    if len(parts) == 1:
        return [pl.BlockSpec((tr, d), lambda i: (i, 0))], None
    na = parts[0].shape[0] // tr
    return [pl.BlockSpec((tr, d), lambda i: (jnp.minimum(i, na - 1), 0)),
            pl.BlockSpec((tr, d), lambda i: (jnp.maximum(i - na, 0), 0))], na


def _stream_block(x_refs, first_blocks):
    if len(x_refs) == 1:
        return x_refs[0][...]
    return jnp.where(pl.program_id(0) < first_blocks, x_refs[0][...], x_refs[1][...])


def _rmsnorm_kernel(*refs, n_parts, first_blocks):
    g_ref, o_ref = refs[n_parts:]
    o_ref[...] = (_rms(_stream_block(refs[:n_parts], first_blocks), NORM_EPS) * g_ref[...]).astype(o_ref.dtype)


def _rmsnorm(parts, g):
    d = parts[0].shape[1]
    t = sum(p.shape[0] for p in parts)
    tr = min(ROW_TILE, min(p.shape[0] for p in parts))
    x_specs, first_blocks = _stream_specs(parts, tr)
    return pl.pallas_call(
        functools.partial(_rmsnorm_kernel, n_parts=len(parts), first_blocks=first_blocks),
        name="rmsnorm", grid=(t // tr,),
        in_specs=x_specs + [pl.BlockSpec((1, d), lambda i: (0, 0))],
        out_specs=pl.BlockSpec((tr, d), lambda i: (i, 0)),
        out_shape=jax.ShapeDtypeStruct((t, d), BF16),
        compiler_params=_cparams(("parallel",)))(*parts, g.reshape(1, d))


def _resid_norm_kernel(*refs, n_parts, first_blocks, scale):
    y_ref, g1_ref, g2_ref, xo_ref, h_ref = refs[n_parts:]
    xn = _stream_block(refs[:n_parts], first_blocks) + scale * (_rms(y_ref[...].astype(F32), NORM_EPS) * g1_ref[...])
    xo_ref[...] = xn
    h_ref[...] = (_rms(xn, NORM_EPS) * g2_ref[...]).astype(h_ref.dtype)


def _resid_kernel(x_ref, y_ref, g1_ref, xo_ref, *, scale):
    xo_ref[...] = x_ref[...] + scale * (_rms(y_ref[...].astype(F32), NORM_EPS) * g1_ref[...])


def _resid_norm(parts, y, g_post, scale, g_next):
    t, d = y.shape
    tr = min(ROW_TILE, min(p.shape[0] for p in parts))
    x_specs, first_blocks = _stream_specs(parts, tr)
    row = pl.BlockSpec((tr, d), lambda i: (i, 0))
    vec = pl.BlockSpec((1, d), lambda i: (0, 0))
    return pl.pallas_call(
        functools.partial(_resid_norm_kernel, n_parts=len(parts), first_blocks=first_blocks, scale=scale),
        name="resid_norm", grid=(t // tr,),
        in_specs=x_specs + [row, vec, vec], out_specs=[row, row],
        out_shape=[jax.ShapeDtypeStruct((t, d), F32), jax.ShapeDtypeStruct((t, d), BF16)],
        compiler_params=_cparams(("parallel",)))(*parts, y, g_post.reshape(1, d), g_next.reshape(1, d))


def _resid_rows(x, y, g_post, scale, row_off, n_rows):
    d = x.shape[1]
    tr = min(ROW_TILE, n_rows)
    assert row_off % tr == 0 and n_rows % tr == 0
    b0 = row_off // tr
    row = pl.BlockSpec((tr, d), lambda i: (b0 + i, 0))
    return pl.pallas_call(
        functools.partial(_resid_kernel, scale=scale), name="resid", grid=(n_rows // tr,),
        in_specs=[row, row, pl.BlockSpec((1, d), lambda i: (0, 0))],
        out_specs=pl.BlockSpec((tr, d), lambda i: (i, 0)),
        out_shape=jax.ShapeDtypeStruct((n_rows, d), F32),
        compiler_params=_cparams(("parallel",)))(x, y, g_post.reshape(1, d))


def _conv_kernel(prev_ref, x_ref, next_ref, w_ref, b_ref, sh_ref, o_ref, *, rows, width, start_tiles, end_tiles):
    i = pl.program_id(0)
    halo = BF16_SUBLANES
    pad = width // 2
    blk = CONV_BLOCK
    is_start = functools.reduce(jnp.logical_or, [i == s for s in start_tiles])
    is_end = functools.reduce(jnp.logical_or, [i == e for e in end_tiles])
    prev = prev_ref[...]
    nxt = next_ref[...]
    ext = jnp.concatenate([jnp.where(is_start, jnp.zeros_like(prev), prev), x_ref[...],
                           jnp.where(is_end, jnp.zeros_like(nxt), nxt)], axis=0)
    w = w_ref[...]
    bias = b_ref[...]
    select = sh_ref[...]
    taps = [k for k in range(width) if k != pad]
    for rb in range(rows // blk):
        shifted = _dot(select, ext[rb * blk:rb * blk + blk + 2 * halo, :])
        acc = bias + w[pad:pad + 1, :] * x_ref[rb * blk:(rb + 1) * blk, :].astype(F32)
        for n, k in enumerate(taps):
            acc = acc + w[k:k + 1, :] * shifted[n * blk:(n + 1) * blk, :]
        o_ref[rb * blk:(rb + 1) * blk, :] = _silu(acc).astype(o_ref.dtype)


def _conv_silu(proj, col_off, conv_w, conv_b, seq_lens):
    t = proj.shape[0]
    width, ch = conv_w.shape
    rows = min(CONV_ROWS, min(seq_lens))
    cols = min(CONV_COLS, ch)
    halo = BF16_SUBLANES
    blk = CONV_BLOCK
    pad = width // 2
    assert col_off % cols == 0 and ch % cols == 0 and all(s % rows == 0 for s in seq_lens)
    assert rows % blk == 0 and pad <= halo
    starts, ends, pos = [], [], 0
    for s in seq_lens:
        starts.append(pos // rows)
        pos += s
        ends.append(pos // rows - 1)
    rb = rows // halo
    nhalo = t // halo
    cb = col_off // cols
    taps = jnp.array([k for k in range(width) if k != pad])
    src = jnp.arange(blk)[None, :] + halo + (taps - pad)[:, None]
    select = (src.reshape(-1)[:, None] == jnp.arange(blk + 2 * halo)[None, :]).astype(BF16)
    return pl.pallas_call(
        functools.partial(_conv_kernel, rows=rows, width=width, start_tiles=tuple(starts), end_tiles=tuple(ends)),
        name="conv_silu", grid=(t // rows, ch // cols),
        in_specs=[pl.BlockSpec((halo, cols), lambda i, j: (jnp.maximum(i * rb - 1, 0), cb + j)),
                  pl.BlockSpec((rows, cols), lambda i, j: (i, cb + j)),
                  pl.BlockSpec((halo, cols), lambda i, j: (jnp.minimum((i + 1) * rb, nhalo - 1), cb + j)),
                  pl.BlockSpec((width, cols), lambda i, j: (0, j)),
                  pl.BlockSpec((1, cols), lambda i, j: (0, j)),
                  pl.BlockSpec(select.shape, lambda i, j: (0, 0))],
        out_specs=pl.BlockSpec((rows, cols), lambda i, j: (i, j)),
        out_shape=jax.ShapeDtypeStruct((t, ch), BF16),
        compiler_params=_cparams(("parallel", "arbitrary")))(
            proj, proj, proj, conv_w, conv_b.reshape(1, ch), select)


def _ssd_kernel(*refs, reverse, n_chunks, reset_chunks, groups, heads, d_inner):
    if reverse:
        (xbc_ref, dt_ref, dtb_ref, alog_ref, tri_ref, trit_ref, exp_ref,
         yf_ref, z_ref, nw_ref, o_ref, state_ref) = refs
    else:
        (xbc_ref, dt_ref, dtb_ref, alog_ref, tri_ref, trit_ref, exp_ref,
         dskip_ref, o_ref, state_ref) = refs
    step = pl.program_id(0)
    chunk = (n_chunks - 1 - step) if reverse else step
    L = SSD_CHUNK
    gw = d_inner // groups
    hpg = heads // groups
    n = SSM_STATE
    col0 = heads if reverse else 0

    @pl.when(functools.reduce(jnp.logical_or, [chunk == c for c in reset_chunks]))
    def _():
        state_ref[...] = jnp.zeros_like(state_ref)

    dt = _softplus(dt_ref[...] + dtb_ref[...])
    dta = dt * (-jnp.exp(alog_ref[...]))
    dt_t = dt.T
    dta_t = dta.T
    tri = tri_ref[...]
    tri_t = trit_ref[...]
    acum = sum(_dot(tri, p) for p in _split_bf16(dta, 3))
    acum_t = sum(_dot(p, tri_t) for p in _split_bf16(dta_t, 3))
    edge = 0 if reverse else L - 1
    a_end = acum[edge:edge + 1, :]
    to_end = jnp.exp(a_end - acum) * dt
    ea = jnp.exp(acum)
    expand = exp_ref[...]
    to_end_x = sum(_dot(p, expand) for p in _split_bf16(to_end, 2))
    ea_x = sum(_dot(p, expand) for p in _split_bf16(ea, 2))
    decay_x = ea_x[edge:edge + 1, :]
    acum2 = acum * LOG2E
    row_t = acum_t * LOG2E - jnp.log2(dt_t)

    row = lax.broadcasted_iota(jnp.int32, (L, L), 0)
    colm = lax.broadcasted_iota(jnp.int32, (L, L), 1)
    causal = (row <= colm) if reverse else (row >= colm)
    lane = lax.broadcasted_iota(jnp.int32, (L, LANES), 1)
    first_half = lane < SSM_HEAD_DIM

    for g in range(groups):
        xs = xbc_ref[:, g * gw:(g + 1) * gw]
        bm = xbc_ref[:, d_inner + g * n:d_inner + (g + 1) * n]
        cm = xbc_ref[:, d_inner + groups * n + g * n:d_inner + groups * n + (g + 1) * n]
        cb = _dot_nt(cm, bm)
        st = state_ref[g]
        y_off = _dot(cm, st.astype(BF16)) * ea_x[:, g * gw:(g + 1) * gw]
        xs_scaled = (xs.astype(F32) * to_end_x[:, g * gw:(g + 1) * gw]).astype(BF16)
        state_ref[g] = st * decay_x[:, g * gw:(g + 1) * gw] + _dot_tn(bm, xs_scaled)
        pairs = []
        for k in range(hpg // 2):
            xp = xs[:, k * LANES:(k + 1) * LANES]
            y_pair = None
            for side in range(2):
                c = col0 + g * hpg + 2 * k + side
                seg = acum2[:, c:c + 1] - row_t[c:c + 1, :]
                m = (cb * jnp.exp2(jnp.where(causal, seg, -jnp.inf))).astype(BF16)
                keep = first_half if side == 0 else jnp.logical_not(first_half)
                part = _dot(m, jnp.where(keep, xp, jnp.zeros_like(xp)))
                y_pair = part if y_pair is None else y_pair + part
            lo = g * gw + k * LANES
            y_pair = y_pair + y_off[:, k * LANES:(k + 1) * LANES]
            if not reverse:
                o_ref[:, lo:lo + LANES] = y_pair + dskip_ref[:, lo:lo + LANES] * xp.astype(F32)
            else:
                pairs.append(y_pair)
        if reverse:
            y = jnp.concatenate(pairs, axis=1) + yf_ref[:, g * gw:(g + 1) * gw]
            y = y * _silu(z_ref[:, g * gw:(g + 1) * gw].astype(F32))
            o_ref[:, g * gw:(g + 1) * gw] = (_rms(y, NORM_EPS) * nw_ref[:, g * gw:(g + 1) * gw]).astype(o_ref.dtype)


def _ssd(xbc, proj, dt_raw, dt_bias, a_log, d_skip, norm_w, seq_lens, groups):
    t = xbc.shape[0]
    heads = d_skip.shape[0]
    d_inner = heads * SSM_HEAD_DIM
    L = SSD_CHUNK
    nc = t // L
    assert 2 * heads <= LANES and (heads // groups) % 2 == 0
    starts, lasts, pos = [], [], 0
    for s in seq_lens:
        assert s % L == 0
        starts.append(pos // L)
        pos += s
        lasts.append(pos // L - 1)
    padh = LANES - 2 * heads
    dtb = jnp.pad(dt_bias.reshape(-1), (0, padh))
    alog = jnp.pad(a_log.reshape(-1), (0, padh))
    tri_lo = jnp.tril(jnp.ones((L, L), F32)).astype(BF16)
    tri_up = jnp.triu(jnp.ones((L, L), F32)).astype(BF16)
    head_of_col = jnp.arange(d_inner) // SSM_HEAD_DIM
    dskip_x = jnp.repeat(d_skip.astype(F32), SSM_HEAD_DIM).reshape(1, d_inner)

    def run(reverse, extra_args, extra_specs, out_dtype):
        cidx = (lambda i: nc - 1 - i) if reverse else (lambda i: i)
        tri = tri_up if reverse else tri_lo
        expand = (jnp.arange(LANES)[:, None] == (head_of_col + (heads if reverse else 0))[None, :]).astype(BF16)
        const = lambda shape: pl.BlockSpec(shape, lambda i: (0, 0))
        in_specs = [pl.BlockSpec((L, xbc.shape[1]), lambda i: (cidx(i), 0)),
                    pl.BlockSpec((L, LANES), lambda i: (cidx(i), 0)),
                    const((1, LANES)), const((1, LANES)),
                    const((L, L)), const((L, L)), const((LANES, d_inner))] + extra_specs(cidx)
        return pl.pallas_call(
            functools.partial(_ssd_kernel, reverse=reverse, n_chunks=nc,
                              reset_chunks=tuple(lasts if reverse else starts),
                              groups=groups, heads=heads, d_inner=d_inner),
            name="ssd_bwd" if reverse else "ssd_fwd", grid=(nc,), in_specs=in_specs,
            out_specs=pl.BlockSpec((L, d_inner), lambda i: (cidx(i), 0)),
            out_shape=jax.ShapeDtypeStruct((t, d_inner), out_dtype),
            scratch_shapes=[pltpu.VMEM((groups, SSM_STATE, d_inner // groups), F32)],
            compiler_params=_cparams(("arbitrary",)))(
                xbc, dt_raw, dtb.reshape(1, LANES), alog.reshape(1, LANES), tri, tri.T, expand, *extra_args)

    y_fwd = run(False, [dskip_x], lambda cidx: [pl.BlockSpec((1, d_inner), lambda i: (0, 0))], F32)
    return run(True, [y_fwd, proj, norm_w.reshape(1, d_inner)],
               lambda cidx: [pl.BlockSpec((L, d_inner), lambda i: (cidx(i), 0)),
                             pl.BlockSpec((L, d_inner), lambda i: (cidx(i), 0)),
                             pl.BlockSpec((1, d_inner), lambda i: (0, 0))], BF16)


def _t5_bucket(rel):
    half = NUM_BUCKETS // 2
    max_exact = half // 2
    n = jnp.abs(rel)
    log_ratio = jnp.log(jnp.maximum(n, 1).astype(F32) / max_exact) / math.log(MAX_DISTANCE / max_exact)
    large = jnp.minimum(max_exact + (log_ratio * (half - max_exact)).astype(jnp.int32), half - 1)
    return jnp.where(rel > 0, half, 0) + jnp.where(n < max_exact, n, large)


def _attn_kernel(far_ref, lq_ref, sw_ref, q_ref, k_ref, v_ref, vt_ref, bias_ref, o_ref, m_ref, l_ref, acc_ref,
                 sa_ref, sb_ref, mt_ref, lt_ref, acct_ref, *, t, nk, lambda_init):
    h = pl.program_id(1)
    i = pl.program_id(2)
    dh = DIFF_HEAD_DIM
    nblk = t // LANES
    c_before = far_ref[h, 0]
    c_after = far_ref[h, 1]
    m_ref[...] = jnp.full(m_ref.shape, -jnp.inf, F32)
    l_ref[...] = jnp.zeros(l_ref.shape, F32)
    acc_ref[...] = jnp.zeros(acc_ref.shape, F32)
    mt_ref[...] = jnp.full(mt_ref.shape, -jnp.inf, F32)
    lt_ref[...] = jnp.zeros(lt_ref.shape, F32)
    acct_ref[...] = jnp.zeros(acct_ref.shape, F32)

    def qk(kb, j, width=1):
        ks = k_ref[pl.ds(pl.multiple_of(kb * t, t), width * t), j * dh:(j + 1) * dh]
        return _dot_nt(q_ref[:, j * dh:(j + 1) * dh], ks)

    def softmax_pv(kb, s_of, shift, width=1):
        vs = v_ref[pl.ds(pl.multiple_of(kb * t, t), width * t), :]
        for j in range(2):
            s = s_of(j)
            blocks = [s[:, c * LANES:(c + 1) * LANES] for c in range(width * nblk)]
            row_max = jnp.max(functools.reduce(jnp.maximum, blocks), axis=-1, keepdims=True)
            m_old = m_ref[j]
            m_new = jnp.maximum(m_old, row_max + shift)
            alpha = jnp.exp2(m_old - m_new)
            ref = m_new - shift
            ps = [jnp.exp2(b - ref) for b in blocks]
            l_ref[j] = alpha * l_ref[j] + functools.reduce(jnp.add, ps)
            p = jnp.concatenate([x.astype(BF16) for x in ps], axis=1)
            acc_ref[j] = jnp.concatenate([alpha] * (acc_ref.shape[2] // LANES), axis=1) * acc_ref[j] + _dot(p, vs)
            m_ref[j] = m_new

    def update(kb, bias_tile):
        softmax_pv(kb, lambda j: qk(kb, j) + bias_tile, 0.0)

    def qk_t(kb, j, width):
        ks = k_ref[pl.ds(pl.multiple_of(kb * t, t), width * t), j * dh:(j + 1) * dh]
        return _dot_nt(ks, q_ref[:, j * dh:(j + 1) * dh])

    def softmax_pv_t(kb, st_ref, shift, width):
        vt = vt_ref[:, pl.ds(pl.multiple_of(kb * t, t), width * t)]
        for j in range(2):
            s = st_ref[j]
            m_old = mt_ref[j]
            m_new = jnp.maximum(m_old, jnp.max(s, axis=0, keepdims=True) + shift)
            alpha = jnp.exp2(m_old - m_new)
            p = jnp.exp2(s - (m_new - shift))
            lt_ref[j] = alpha * lt_ref[j] + jnp.sum(p, axis=0, keepdims=True)
            acct_ref[j] = alpha * acct_ref[j] + _dot(vt, p.astype(BF16))
            mt_ref[j] = m_new

    def far_loop(lo, hi, shift):
        w = FAR_W
        n = hi - lo
        nd = n // w
        base = lo + n % w

        def single(kb, carry):
            softmax_pv(kb, lambda j: qk(kb, j), shift)
            return carry

        lax.fori_loop(lo, base, single, 0)

        @pl.when(nd > 0)
        def _():
            for j in range(2):
                sa_ref[j] = qk_t(base, j, w)

            def pair(kp, carry):
                d0 = base + 2 * w * kp
                d1 = jnp.minimum(d0 + w, hi - w)
                d2 = jnp.minimum(d0 + 2 * w, hi - w)
                for j in range(2):
                    sb_ref[j] = qk_t(d1, j, w)
                softmax_pv_t(d0, sa_ref, shift, w)
                for j in range(2):
                    sa_ref[j] = qk_t(d2, j, w)
                softmax_pv_t(d1, sb_ref, shift, w)
                return carry

            lax.fori_loop(0, nd // 2, pair, 0)

            @pl.when(nd % 2 == 1)
            def _():
                softmax_pv_t(hi - w, sa_ref, shift, w)

    far_loop(0, jnp.maximum(i - 1, 0), c_before)

    @pl.when(i >= 1)
    def _():
        update(i - 1, bias_ref[0, 0])

    update(i, bias_ref[0, 1])

    @pl.when(i + 1 < nk)
    def _():
        update(i + 1, bias_ref[0, 2])

    far_loop(jnp.minimum(i + 2, nk), nk, c_after)

    lq = lq_ref[...]
    lam = (jnp.exp(jnp.sum(lq[0:1] * lq[1:2], axis=-1, keepdims=True))
           - jnp.exp(jnp.sum(lq[2:3] * lq[3:4], axis=-1, keepdims=True)) + lambda_init)
    outs = []
    for j in range(2):
        m_row = m_ref[j].T[0:1, :]
        l_row = jnp.sum(l_ref[j].T, axis=0, keepdims=True)
        m_tot = jnp.maximum(m_row, mt_ref[j])
        a_row = jnp.exp2(m_row - m_tot)
        a_col = jnp.exp2(mt_ref[j] - m_tot)
        l_tot = a_row * l_row + a_col * lt_ref[j]
        outs.append((a_row * acc_ref[j].T + a_col * acct_ref[j]) / l_tot)
    o = (outs[0] - lam * outs[1]).T
    o_ref[...] = (_rms(o, SUBLN_EPS) * sw_ref[...] * (1.0 - lambda_init)).astype(o_ref.dtype)


def _near_bias_tiles(table_t, t):
    x = jnp.concatenate([jnp.arange(t), jnp.arange(t) - t])
    rel = (jnp.arange(-1, 2) * t)[:, None] + x[None, :]
    w = table_t[:, _t5_bucket(rel)]
    flat = jnp.tile(w, (1, 1, t))[..., :t * (2 * t - 1)]
    return flat.reshape(w.shape[0], 3, t, 2 * t - 1)[..., :t]


def _diff_attention(proj, q_off, n_heads, rel_bias, lambda_qk, subln_w, lambda_init, tok_off, bsz, seq):
    t = min(ATTN_TILE, seq)
    nk = seq // t
    hw = 2 * DIFF_HEAD_DIM
    assert t >= MAX_DISTANCE and seq % t == 0 and tok_off % seq == 0 and q_off % hw == 0
    qb = q_off // hw
    table_t = rel_bias.astype(F32).T * LOG2E
    bias_near = _near_bias_tiles(table_t, t)
    far = table_t[:, _t5_bucket(jnp.array([-2 * MAX_DISTANCE, 2 * MAX_DISTANCE]))]
    row0 = tok_off // t
    seq0 = tok_off // seq
    return pl.pallas_call(
        functools.partial(_attn_kernel, t=t, nk=nk, lambda_init=lambda_init),
        name="diff_attn", grid=(bsz, n_heads, nk),
        in_specs=[pl.BlockSpec(memory_space=pltpu.SMEM),
                  pl.BlockSpec((4, DIFF_HEAD_DIM), lambda b, h, i: (0, 0)),
                  pl.BlockSpec((1, hw), lambda b, h, i: (0, 0)),
                  pl.BlockSpec((t, hw), lambda b, h, i: (row0 + b * nk + i, qb + h)),
                  pl.BlockSpec((seq, hw), lambda b, h, i: (seq0 + b, qb + n_heads + h),
                               pipeline_mode=pl.Buffered(1)),
                  pl.BlockSpec((seq, hw), lambda b, h, i: (seq0 + b, qb + 2 * n_heads + h),
                               pipeline_mode=pl.Buffered(1)),
                  pl.BlockSpec((hw, seq), lambda b, h, i: (h, seq0 + b), pipeline_mode=pl.Buffered(1)),
                  pl.BlockSpec((1, 3, t, t), lambda b, h, i: (h, 0, 0, 0))],
        out_specs=pl.BlockSpec((t, hw), lambda b, h, i: (b * nk + i, h)),
        out_shape=jax.ShapeDtypeStruct((bsz * seq, n_heads * hw), BF16),
        scratch_shapes=[pltpu.VMEM((2, t, LANES), F32), pltpu.VMEM((2, t, LANES), F32),
                        pltpu.VMEM((2, t, hw), F32), pltpu.VMEM((2, FAR_W * t, t), F32),
                        pltpu.VMEM((2, FAR_W * t, t), F32),
                        pltpu.VMEM((2, 1, t), F32), pltpu.VMEM((2, 1, t), F32), pltpu.VMEM((2, hw, t), F32)],
        compiler_params=_cparams(("parallel", "parallel", "arbitrary"), MERGE_VMEM_LIMIT_BYTES))(
            far, lambda_qk.astype(F32), subln_w.reshape(1, hw).astype(F32), proj, proj, proj,
            proj[:, q_off + 2 * n_heads * hw:q_off + 3 * n_heads * hw].T, bias_near)


def _gelu_tanh(x):
    return 0.5 * x * (1.0 + jnp.tanh(math.sqrt(2.0 / math.pi) * (x + 0.044715 * (x * x * x))))


def _gmlp_kernel(u_ref, v_ref, lng_ref, lnb_ref, ws_ref, bs_ref, o_ref, *, rows, groups):
    u = _gelu_tanh(u_ref[...].astype(F32))
    v = _gelu_tanh(v_ref[...].astype(F32))
    mu = jnp.mean(v, axis=-1, keepdims=True)
    vc = v - mu
    vn = (vc * lax.rsqrt(jnp.mean(vc * vc, axis=-1, keepdims=True) + LN_EPS) * lng_ref[...] + lnb_ref[...]).astype(BF16)
    gd = v.shape[1] // groups
    L = GMLP_CHUNK
    for c in range(rows // L):
        for g in range(groups):
            mixed = _dot(ws_ref[g], vn[c * L:(c + 1) * L, g * gd:(g + 1) * gd]) + bs_ref[:, g * gd:(g + 1) * gd]
            o_ref[c * L:(c + 1) * L, g * gd:(g + 1) * gd] = (
                u[c * L:(c + 1) * L, g * gd:(g + 1) * gd] * mixed).astype(o_ref.dtype)


def _gmlp(proj, u_off, width, ln_v, w_s, b_s):
    t = proj.shape[0]
    groups = w_s.shape[0]
    L = GMLP_CHUNK
    rows = min(GMLP_ROWS, t)
    assert u_off % width == 0 and rows % L == 0
    ub = u_off // width
    bias_x = jnp.repeat(b_s.astype(F32).T, width // groups, axis=1)
    vec = pl.BlockSpec((1, width), lambda i: (0, 0))
    return pl.pallas_call(
        functools.partial(_gmlp_kernel, rows=rows, groups=groups), name="gmlp", grid=(t // rows,),
        in_specs=[pl.BlockSpec((rows, width), lambda i: (i, ub)),
                  pl.BlockSpec((rows, width), lambda i: (i, ub + 1)),
                  vec, vec,
                  pl.BlockSpec((groups, L, L), lambda i: (0, 0, 0)),
                  pl.BlockSpec((L, width), lambda i: (0, 0))],
        out_specs=pl.BlockSpec((rows, width), lambda i: (i, 0)),
        out_shape=jax.ShapeDtypeStruct((t, width), BF16),
        compiler_params=_cparams(("parallel",)))(
            proj, proj, ln_v[0].reshape(1, width).astype(F32), ln_v[1].reshape(1, width).astype(F32),
            w_s.astype(BF16), bias_x)


def _merge_kernel(h_ref, s_ref, a0_ref, a1_ref, m_ref, wg0_ref, wg1_ref, wg2_ref, wos_ref, woa_ref, wom_ref,
                  o_ref, *, first_blocks):
    def gated_sum(a_ref):
        h = h_ref[...]
        merged = (jax.nn.sigmoid(_dot(h, wg0_ref[...])) * _dot(s_ref[...], wos_ref[...])
                  + jax.nn.sigmoid(_dot(h, wg1_ref[...])) * _dot(a_ref[...], woa_ref[...])
                  + jax.nn.sigmoid(_dot(h, wg2_ref[...])) * _dot(m_ref[...], wom_ref[...]))
        o_ref[...] = merged.astype(o_ref.dtype)

    in_first = pl.program_id(0) < first_blocks
    pl.when(in_first)(lambda: gated_sum(a0_ref))
    pl.when(jnp.logical_not(in_first))(lambda: gated_sum(a1_ref))


def _merge(h, y_s, y_a_parts, y_m, w_gate, w_os, w_oa, w_om):
    t, d = h.shape
    ya0, ya1 = y_a_parts
    tm = min(MERGE_TM, ya0.shape[0], ya1.shape[0])
    tn = min(MERGE_TN, d)
    nj = d // tn
    na = ya0.shape[0] // tm
    assert ya0.shape[0] % tm == 0 and ya1.shape[0] % tm == 0
    act = lambda a: pl.BlockSpec((tm, a.shape[1]), lambda i, j: (i, 0))
    wcol = lambda w, off: pl.BlockSpec((w.shape[0], tn), lambda i, j: (0, j + off))
    return pl.pallas_call(
        functools.partial(_merge_kernel, first_blocks=na), name="merge", grid=(t // tm, nj),
        in_specs=[act(h), act(y_s),
                  pl.BlockSpec((tm, ya0.shape[1]), lambda i, j: (jnp.minimum(i, na - 1), 0)),
                  pl.BlockSpec((tm, ya1.shape[1]), lambda i, j: (jnp.maximum(i - na, 0), 0)),
                  act(y_m),
                  wcol(w_gate, 0), wcol(w_gate, nj), wcol(w_gate, 2 * nj),
                  wcol(w_os, 0), wcol(w_oa, 0), wcol(w_om, 0)],
        out_specs=pl.BlockSpec((tm, tn), lambda i, j: (i, j)),
        out_shape=jax.ShapeDtypeStruct((t, d), BF16),
        compiler_params=_cparams(("parallel", "arbitrary"), MERGE_VMEM_LIMIT_BYTES))(
            h, y_s, ya0, ya1, y_m, w_gate, w_gate, w_gate, w_os, w_oa, w_om)


def kernel(x_prompt, x_sample, rel_bias, norms, w_ffn1_in, w_ffn1_out, w_in, conv_w, conv_b, dt_bias, a_log,
           d_skip, ssm_norm, w_o_ssm, lambda_qk, diff_subln, w_o_diff, ln_v, w_spatial, b_spatial, w_o_gmlp,
           w_out, w_ffn2_in, w_ffn2_out):
    depth = norms.shape[0]
    d = x_prompt.shape[-1]
    d_inner = w_o_ssm.shape[1]
    conv_ch = conv_w.shape[2]
    heads = d_skip.shape[1]
    groups = (conv_ch - d_inner) // (2 * SSM_STATE)
    diff_w = w_o_diff.shape[1]
    diff_heads = diff_w // (2 * DIFF_HEAD_DIM)
    gmlp_w = w_o_gmlp.shape[1]
    pb, ps = x_prompt.shape[:2]
    sb, ss = x_sample.shape[:2]
    seq_lens = [ps] * pb + [ss] * sb
    n_prompt = pb * ps

    c_dt = d_inner + conv_ch
    c_diff = c_dt + 2 * heads
    c_gmlp = c_diff + 3 * diff_w
    c_gate = c_gmlp + 2 * gmlp_w
    q_off = c_dt
    u_off = q_off + 3 * diff_w
    n_main = u_off + 2 * gmlp_w
    col_scale = jnp.ones((1, n_main), F32).at[:, q_off:q_off + diff_w].set(DIFF_HEAD_DIM ** -0.5 * LOG2E)

    x = [x_prompt.reshape(n_prompt, d), x_sample.reshape(sb * ss, d)]
    h = _rmsnorm(x, norms[0, 0])
    for l in range(depth):
        n = norms[l]
        lambda_init = 0.8 - 0.6 * math.exp(-0.3 * l)
        wi = w_in[l]
        w_main = jnp.concatenate([wi[:, :c_dt], wi[:, c_diff:c_gate]], axis=1).astype(BF16)
        w_dt = jnp.pad(wi[:, c_dt:c_diff], ((0, 0), (0, LANES - 2 * heads))).astype(BF16)
        w_gate = wi[:, c_gate:].astype(BF16)

        y = _matmul(_matmul_swiglu(h, w_ffn1_in[l].astype(BF16)), w_ffn1_out[l].astype(BF16), BF16, tn=FFN_OUT_TN)
        x, h = _resid_norm(x, y, n[1], 0.5, n[2])
        x = [x]

        proj = _matmul(h, w_main, BF16, scale=col_scale)
        dt_raw = _matmul(h, w_dt, F32, tn=LANES)
        xbc = _conv_silu(proj, d_inner, conv_w[l], conv_b[l], seq_lens)
        y_ssm = _ssd(xbc, proj, dt_raw, dt_bias[l], a_log[l], d_skip[l], ssm_norm[l], seq_lens, groups)
        y_att = [
            _diff_attention(proj, q_off, diff_heads, rel_bias, lambda_qk[l], diff_subln[l], lambda_init, 0, pb, ps),
            _diff_attention(proj, q_off, diff_heads, rel_bias, lambda_qk[l], diff_subln[l], lambda_init,
                            n_prompt, sb, ss)]
        y_gmlp = _gmlp(proj, u_off, gmlp_w, ln_v[l], w_spatial[l], b_spatial[l])
        merged = _merge(h, y_ssm, y_att, y_gmlp, w_gate, w_o_ssm[l].astype(BF16), w_o_diff[l].astype(BF16),
                        w_o_gmlp[l].astype(BF16))
        y = _matmul(merged, w_out[l].astype(BF16), BF16)
        x, h = _resid_norm(x, y, n[3], 1.0, n[4])
        x = [x]

        y = _matmul(_matmul_swiglu(h, w_ffn2_in[l].astype(BF16)), w_ffn2_out[l].astype(BF16), BF16, tn=FFN_OUT_TN)
        if l + 1 < depth:
            x, h = _resid_norm(x, y, n[5], 0.5, norms[l + 1, 0])
            x = [x]

    y_prompt = _resid_rows(x[0], y, norms[depth - 1, 5], 0.5, 0, n_prompt)
    y_sample = _resid_rows(x[0], y, norms[depth - 1, 5], 0.5, n_prompt, sb * ss)
    return y_prompt.reshape(x_prompt.shape), y_sample.reshape(x_sample.shape)
```
